```python
import math
import jax
import jax.numpy as jnp
from jax import lax
import numpy as np

D_MODEL = 1024
BATCH = 16
SEQ = 2048
DEPTH = 1
DEC_BATCH = 128
DEC_SEQ = 8
PAST_LEN = 8192
PAGE_SIZE = 128

A_GROUPS = ((128, 1), (512, 4), (2048, 16))
A_N_GROUPS = len(A_GROUPS)
A_HEADS = 4
A_HEAD_DIM = 64
A_WIDTH = A_N_GROUPS * A_HEADS * A_HEAD_DIM
A_OUT = A_HEADS * A_HEAD_DIM
B_HEADS = 4
B_DK = D_MODEL // 16
B_DV = D_MODEL // 8
B_KW = B_HEADS * B_DK
B_VW = B_HEADS * B_DV
B_RANK = 16
B_TAU = 16.0
B_CHUNK = 64
C_HEADS = 4
C_HEAD_DIM = D_MODEL // 8
C_WIDTH = C_HEADS * C_HEAD_DIM
N_MEM = 256
N_BRANCH = 3
IN_SPLITS = (A_WIDTH, A_WIDTH, A_WIDTH, B_KW, B_KW, B_VW, B_VW, B_RANK, C_WIDTH, N_BRANCH * D_MODEL)
IN_WIDTH = sum(IN_SPLITS)
N_EXPERT_GROUPS = 4
EXPERTS_PER_GROUP = 8
N_EXPERTS = N_EXPERT_GROUPS * EXPERTS_PER_GROUP
TOP_K = 2
D_EXPERT = D_MODEL // 4
MOE_BLOCK = 128
EPS = 1e-6

kernel_name = 'hybrid_dilated_gla_mem_hmoe_step'


def _rmsnorm(x, g):
    xf = x.astype(jnp.float32)
    y = xf * lax.rsqrt(jnp.mean(xf * xf, axis=-1, keepdims=True) + EPS)
    return (y * g.astype(jnp.float32)).astype(x.dtype)


def _mixer_inputs(x, norm_mix, w_in, qn_a, kn_a, qn_c, gla_gate_up, gla_gate_bias):
    Bx, T, _ = x.shape
    xn = _rmsnorm(x, norm_mix)
    h = jnp.einsum('btd,de->bte', xn, w_in)
    offsets = [int(o) for o in np.cumsum(IN_SPLITS)[:-1]]
    qa, ka, va, qb, kb, vb, rb, ab, qc, gl = jnp.split(h, offsets, axis=-1)
    shp_a = (Bx, T, A_N_GROUPS, A_HEADS, A_HEAD_DIM)
    qa = _rmsnorm(qa.reshape(shp_a), qn_a) * (A_HEAD_DIM ** -0.5)
    ka = _rmsnorm(ka.reshape(shp_a), kn_a)
    va = va.reshape(shp_a)
    qb = qb.reshape(Bx, T, B_HEADS, B_DK) * (B_DK ** -0.5)
    kb = kb.reshape(Bx, T, B_HEADS, B_DK)
    vb = vb.reshape(Bx, T, B_HEADS, B_DV)
    z = jnp.einsum('btr,rk->btk', ab, gla_gate_up).astype(jnp.float32) + gla_gate_bias.astype(jnp.float32)
    log_a = (jax.nn.log_sigmoid(z) / B_TAU).reshape(Bx, T, B_HEADS, B_DK)
    qc = _rmsnorm(qc.reshape(Bx, T, C_HEADS, C_HEAD_DIM), qn_c) * (C_HEAD_DIM ** -0.5)
    gates = jax.nn.sigmoid(gl.astype(jnp.float32)).reshape(Bx, T, N_BRANCH, D_MODEL)
    return qa, ka, va, qb, kb, vb, log_a, rb, qc, gates


def _dilated_band(q, k, v, J, r):
    B, S, H, D = q.shape
    M = S // r
    nb = -(-M // J)
    Mp = nb * J

    def sub(t):
        t = jnp.swapaxes(t.reshape(B, M, r, H, D), 1, 2)
        t = jnp.pad(t, ((0, 0), (0, 0), (0, Mp - M), (0, 0), (0, 0)))
        return t.reshape(B, r, nb, J, H, D)

    def band(t):
        prev = jnp.pad(t, ((0, 0), (0, 0), (1, 0), (0, 0), (0, 0), (0, 0)))[:, :, :-1]
        return jnp.concatenate([prev, t], axis=3)

    qb, kb, vb = sub(q), band(sub(k)), band(sub(v))
    s = jnp.einsum('brnqhd,brnkhd->brnhqk', qb, kb).astype(jnp.float32)
    qi = jnp.arange(J)[:, None] + J
    ki = jnp.arange(2 * J)[None, :]
    blk = jnp.arange(nb)[:, None, None]
    valid = (ki <= qi) & (ki >= qi - J) & (blk * J + ki[None] - J >= 0)
    s = jnp.where(valid[None, None, :, None], s, -jnp.inf)
    m = jnp.max(s, axis=-1, keepdims=True)
    p = jnp.exp(s - m)
    l = jnp.sum(p, axis=-1)
    o = jnp.einsum('brnhqk,brnkhd->brnqhd', p, vb.astype(jnp.float32))
    o = o / jnp.swapaxes(l, 3, 4)[..., None]
    lse = jnp.swapaxes(m[..., 0] + jnp.log(l), 3, 4)

    def unsub(t):
        t = t.reshape((B, r, Mp) + t.shape[4:])[:, :, :M]
        t = jnp.swapaxes(t, 1, 2)
        return t.reshape((B, S) + t.shape[3:])

    return unsub(o), unsub(lse)


def _combine_groups(outs, lses):
    o = jnp.stack(outs, axis=0)
    wts = jax.nn.softmax(jnp.stack(lses, axis=0), axis=0)
    return jnp.sum(wts[..., None] * o, axis=0)


def _dilated_prompt(q, k, v):
    T = q.shape[1]
    outs, lses, bufs = [], [], []
    for g, (w, r) in enumerate(A_GROUPS):
        o, lse = _dilated_band(q[:, :, g], k[:, :, g], v[:, :, g], w // r, r)
        outs.append(o)
        lses.append(lse)
        wb = min(w, T)
        bufs.append(jnp.stack([k[:, T - wb:, g], v[:, T - wb:, g]], axis=2))
    return _combine_groups(outs, lses), bufs


def _dilated_sample(q, k, v, bufs):
    L = q.shape[1]
    outs, lses, new_bufs = [], [], []
    for g, ((w, r), buf) in enumerate(zip(A_GROUPS, bufs)):
        Wb = buf.shape[1]
        kv_new = jnp.stack([k[:, :, g], v[:, :, g]], axis=2)
        kvc = jnp.concatenate([buf.astype(kv_new.dtype), kv_new], axis=1)
        J = w // r
        idx = Wb + jnp.arange(L)[:, None] - r * jnp.arange(J + 1)[None, :]
        valid = idx >= 0
        kvg = kvc[:, jnp.maximum(idx, 0)]
        s = jnp.einsum('blhd,bljhd->blhj', q[:, :, g], kvg[:, :, :, 0]).astype(jnp.float32)
        s = jnp.where(valid[None, :, None, :], s, -jnp.inf)
        m = jnp.max(s, axis=-1, keepdims=True)
        p = jnp.exp(s - m)
        l = jnp.sum(p, axis=-1)
        o = jnp.einsum('blhj,bljhd->blhd', p, kvg[:, :, :, 1].astype(jnp.float32)) / l[..., None]
        outs.append(o)
        lses.append(m[..., 0] + jnp.log(l))
        new_bufs.append(kvc[:, kvc.shape[1] - Wb:])
    return _combine_groups(outs, lses), new_bufs


def _gla(q, k, v, log_a, s0):
    B, T, H, dk = q.shape
    C = math.gcd(T, B_CHUNK)
    n = T // C

    def to_chunks(t):
        return t.astype(jnp.float32).reshape(B, n, C, H, t.shape[-1]).transpose(1, 0, 3, 2, 4)

    causal = jnp.tril(jnp.ones((C, C), dtype=bool))

    def step(S, inp):
        qc, kc, vc, ac = inp
        b = jnp.cumsum(ac, axis=2)
        o_inter = jnp.einsum('bhcd,bhdv->bhcv', qc * jnp.exp(b), S)
        diff = b[:, :, :, None, :] - b[:, :, None, :, :]
        decay = jnp.exp(jnp.where(causal[None, None, :, :, None], diff, -jnp.inf))
        att = jnp.einsum('bhtd,bhsd,bhtsd->bhts', qc, kc, decay)
        o_intra = jnp.einsum('bhts,bhsv->bhtv', att, vc)
        b_last = b[:, :, -1:, :]
        S_new = jnp.exp(b_last[:, :, 0, :])[..., None] * S + jnp.einsum('bhsd,bhsv->bhdv', kc * jnp.exp(b_last - b), vc)
        return S_new, o_inter + o_intra

    S_fin, o = lax.scan(step, s0.astype(jnp.float32), (to_chunks(q), to_chunks(k), to_chunks(v), to_chunks(log_a)))
    o = o.transpose(1, 0, 3, 2, 4).reshape(B, T, H, v.shape[-1])
    return o, S_fin


def _mem_kv(mem, mem_norm, w_mem_kv, kn_c):
    Bm, N, _ = mem.shape
    kv = jnp.einsum('bnd,de->bne', _rmsnorm(mem, mem_norm), w_mem_kv).reshape(Bm, N, 2, C_HEADS, C_HEAD_DIM)
    k = _rmsnorm(kv[:, :, 0], kn_c)
    return jnp.stack([k, kv[:, :, 1]], axis=2)


def _cross(q, mem_kv):
    s = jnp.einsum('bthd,bnhd->bhtn', q, mem_kv[:, :, 0].astype(q.dtype)).astype(jnp.float32)
    p = jax.nn.softmax(s, axis=-1)
    return jnp.einsum('bhtn,bnhd->bthd', p, mem_kv[:, :, 1].astype(jnp.float32))


def _expert_dispatch(xt, eidx, ew, w_exp_gate, w_exp_up, w_exp_down):
    N, D = xt.shape
    A = N * TOP_K
    flat_e = eidx.reshape(-1)
    order = jnp.argsort(flat_e)
    se = flat_e[order]
    tok = order // TOP_K
    sw = ew.reshape(-1)[order]
    counts = jnp.bincount(flat_e, length=N_EXPERTS)
    padded = (counts + MOE_BLOCK - 1) // MOE_BLOCK * MOE_BLOCK
    pad_end = jnp.cumsum(padded)
    pad_start = pad_end - padded
    start = jnp.cumsum(counts) - counts
    dest = pad_start[se] + jnp.arange(A) - start[se]
    n_blocks = -(-A // MOE_BLOCK) + N_EXPERTS
    P = n_blocks * MOE_BLOCK
    xs = jnp.zeros((P, D), xt.dtype).at[dest].set(xt[tok])
    block_e = jnp.minimum(jnp.searchsorted(pad_end, jnp.arange(n_blocks) * MOE_BLOCK, side='right'), N_EXPERTS - 1)

    def run(args):
        xb, e = args
        hid = jax.nn.silu(xb @ w_exp_gate[e]) * (xb @ w_exp_up[e])
        return hid @ w_exp_down[e]

    ys = lax.map(run, (xs.reshape(n_blocks, MOE_BLOCK, D), block_e))
    yt = ys.reshape(P, D)[dest] * sw[:, None].astype(xt.dtype)
    return jnp.zeros((N, D), xt.dtype).at[tok].add(yt)


def _hmoe(h, norm_ffn, w_router_group, b_router_group, w_router_expert, b_router_expert, w_exp_gate, w_exp_up, w_exp_down):
    Bx, T, D = h.shape
    xt = _rmsnorm(h, norm_ffn).reshape(Bx * T, D)
    g_logit = (xt @ w_router_group).astype(jnp.float32) + b_router_group.astype(jnp.float32)
    g_prob = jax.nn.softmax(g_logit, axis=-1)
    grp = jnp.argmax(g_logit, axis=-1)
    g_w = jnp.take_along_axis(g_prob, grp[:, None], axis=-1)
    e_logit = (xt @ w_router_expert).astype(jnp.float32) + b_router_expert.astype(jnp.float32)
    e_logit = e_logit.reshape(-1, N_EXPERT_GROUPS, EXPERTS_PER_GROUP)
    e_logit = jnp.take_along_axis(e_logit, grp[:, None, None], axis=1)[:, 0]
    top_v, top_i = lax.top_k(e_logit, TOP_K)
    w = jax.nn.softmax(top_v, axis=-1) * g_w
    eidx = grp[:, None] * EXPERTS_PER_GROUP + top_i
    y = _expert_dispatch(xt, eidx, w, w_exp_gate, w_exp_up, w_exp_down)
    return h + y.reshape(Bx, T, D)


def _merge_and_ffn(x, oa, ob, rb, oc, gates, gla_norm, w_branch_a, w_branch_b, w_branch_c, w_out,
                   norm_ffn, w_router_group, b_router_group, w_router_expert, b_router_expert,
                   w_exp_gate, w_exp_up, w_exp_down):
    Bx, T, _ = x.shape
    mu = jnp.mean(ob, axis=-1, keepdims=True)
    var = jnp.mean(jnp.square(ob - mu), axis=-1, keepdims=True)
    obn = (ob - mu) * lax.rsqrt(var + EPS) * gla_norm.astype(jnp.float32)
    ob2 = obn.reshape(Bx, T, B_VW) * jax.nn.silu(rb.astype(jnp.float32))
    pa = jnp.einsum('btk,kd->btd', oa.reshape(Bx, T, A_OUT).astype(x.dtype), w_branch_a)
    pb = jnp.einsum('btk,kd->btd', ob2.astype(x.dtype), w_branch_b)
    pc = jnp.einsum('btk,kd->btd', oc.reshape(Bx, T, C_WIDTH).astype(x.dtype), w_branch_c)
    merged = gates[:, :, 0] * pa + gates[:, :, 1] * pb + gates[:, :, 2] * pc
    h = x + jnp.einsum('btd,de->bte', merged.astype(x.dtype), w_out)
    return _hmoe(h, norm_ffn, w_router_group, b_router_group, w_router_expert, b_router_expert, w_exp_gate, w_exp_up, w_exp_down)


def setup_inputs(seed: int = 0) -> dict:
    key = jax.random.key(seed)
    keys = iter(jax.random.split(key, 40))

    def nrm(shape, scale=1.0):
        return jax.random.normal(next(keys), shape, jnp.float32) * scale

    def gain(n):
        return 1.0 + nrm((n,), 0.01)

    wb = [min(w, PAST_LEN) for w, _ in A_GROUPS]
    return {
        'x_prompt': nrm((BATCH, SEQ, D_MODEL)),
        'x_sample': nrm((DEC_BATCH, DEC_SEQ, D_MODEL)),
        'mem_prompt': nrm((BATCH, N_MEM, D_MODEL)),
        'cache_win1': nrm((DEC_BATCH, wb[0], 2, A_HEADS, A_HEAD_DIM)),
        'cache_win2': nrm((DEC_BATCH, wb[1], 2, A_HEADS, A_HEAD_DIM)),
        'cache_win3': nrm((DEC_BATCH, wb[2], 2, A_HEADS, A_HEAD_DIM)),
        'state_gla': nrm((DEC_BATCH, B_HEADS, B_DK, B_DV)),
        'cache_mem': nrm((DEC_BATCH, N_MEM, 2, C_HEADS, C_HEAD_DIM)),
        'norm_mix': gain(D_MODEL),
        'w_in': nrm((D_MODEL, IN_WIDTH), D_MODEL ** -0.5),
        'qn_a': gain(A_HEAD_DIM),
        'kn_a': gain(A_HEAD_DIM),
        'qn_c': gain(C_HEAD_DIM),
        'kn_c': gain(C_HEAD_DIM),
        'gla_gate_up': nrm((B_RANK, B_KW), B_RANK ** -0.5),
        'gla_gate_bias': nrm((B_KW,), 0.1),
        'gla_norm': gain(B_DV),
        'mem_norm': gain(D_MODEL),
        'w_mem_kv': nrm((D_MODEL, 2 * C_WIDTH), D_MODEL ** -0.5),
        'w_branch_a': nrm((A_OUT, D_MODEL), A_OUT ** -0.5),
        'w_branch_b': nrm((B_VW, D_MODEL), B_VW ** -0.5),
        'w_branch_c': nrm((C_WIDTH, D_MODEL), C_WIDTH ** -0.5),
        'w_out': nrm((D_MODEL, D_MODEL), D_MODEL ** -0.5),
        'norm_ffn': gain(D_MODEL),
        'w_router_group': nrm((D_MODEL, N_EXPERT_GROUPS), D_MODEL ** -0.5),
        'b_router_group': nrm((N_EXPERT_GROUPS,), 0.01),
        'w_router_expert': nrm((D_MODEL, N_EXPERTS), D_MODEL ** -0.5),
        'b_router_expert': nrm((N_EXPERTS,), 0.01),
        'w_exp_gate': nrm((N_EXPERTS, D_MODEL, D_EXPERT), D_MODEL ** -0.5),
        'w_exp_up': nrm((N_EXPERTS, D_MODEL, D_EXPERT), D_MODEL ** -0.5),
        'w_exp_down': nrm((N_EXPERTS, D_EXPERT, D_MODEL), D_EXPERT ** -0.5),
    }


def reference(x_prompt, x_sample, mem_prompt, cache_win1, cache_win2, cache_win3, state_gla, cache_mem,
              norm_mix, w_in, qn_a, kn_a, qn_c, kn_c, gla_gate_up, gla_gate_bias, gla_norm, mem_norm,
              w_mem_kv, w_branch_a, w_branch_b, w_branch_c, w_out, norm_ffn, w_router_group, b_router_group,
              w_router_expert, b_router_expert, w_exp_gate, w_exp_up, w_exp_down):
    qa, ka, va, qb, kb, vb, log_a, rb, qc, gates = _mixer_inputs(
        x_prompt, norm_mix, w_in, qn_a, kn_a, qn_c, gla_gate_up, gla_gate_bias)
    oa, (win1_p, win2_p, win3_p) = _dilated_prompt(qa, ka, va)
    s0 = jnp.zeros((x_prompt.shape[0], B_HEADS, B_DK, B_DV), jnp.float32)
    ob, gla_p = _gla(qb, kb, vb, log_a, s0)
    mem_p = _mem_kv(mem_prompt, mem_norm, w_mem_kv, kn_c)
    oc = _cross(qc, mem_p)
    y_prompt = _merge_and_ffn(x_prompt, oa, ob, rb, oc, gates, gla_norm, w_branch_a, w_branch_b, w_branch_c,
                              w_out, norm_ffn, w_router_group, b_router_group, w_router_expert,
                              b_router_expert, w_exp_gate, w_exp_up, w_exp_down)
    qa, ka, va, qb, kb, vb, log_a, rb, qc, gates = _mixer_inputs(
        x_sample, norm_mix, w_in, qn_a, kn_a, qn_c, gla_gate_up, gla_gate_bias)
    oa, (win1_s, win2_s, win3_s) = _dilated_sample(qa, ka, va, (cache_win1, cache_win2, cache_win3))
    ob, gla_s = _gla(qb, kb, vb, log_a, state_gla)
    oc = _cross(qc, cache_mem)
    y_sample = _merge_and_ffn(x_sample, oa, ob, rb, oc, gates, gla_norm, w_branch_a, w_branch_b, w_branch_c,
                              w_out, norm_ffn, w_router_group, b_router_group, w_router_expert,
                              b_router_expert, w_exp_gate, w_exp_up, w_exp_down)
    return (y_prompt, y_sample, win1_p, win2_p, win3_p, gla_p, mem_p, win1_s, win2_s, win3_s, gla_s)
```

```python
import functools

import jax
import jax.numpy as jnp
import numpy as np
from jax import lax
from jax.experimental import pallas as pl
from jax.experimental.pallas import tpu as pltpu

F32 = jnp.float32
BF16 = jnp.bfloat16
HIGHEST = lax.Precision.HIGHEST

D_MODEL = 1024
A_GROUPS = ((128, 1), (512, 4), (2048, 16))
A_J = 128
A_HEADS = 4
A_HEAD_DIM = 64
A_GW = A_HEADS * A_HEAD_DIM
A_WIDTH = 3 * A_GW
B_HEADS = 4
B_DK = 64
B_DV = 128
B_KW = B_HEADS * B_DK
B_VW = B_HEADS * B_DV
B_RANK = 16
B_TAU = 16.0
B_CHUNK = 64
C_HEADS = 4
C_HEAD_DIM = 128
C_WIDTH = C_HEADS * C_HEAD_DIM
N_MEM = 256
N_GROUPS_E = 4
E_PER_GROUP = 8
N_EXPERTS = 32
D_EXPERT = 256
EPS = 1e-6
NEG = -1e30
LANES = 128
VMEM_LIMIT = 56 * 1024 * 1024

NT = (((1,), (1,)), ((), ()))
TN = (((0,), (0,)), ((), ()))


def _cparams(sem):
    return pltpu.CompilerParams(dimension_semantics=sem, vmem_limit_bytes=VMEM_LIMIT)


def _const_spec(shape):
    nd = len(shape)
    return pl.BlockSpec(shape, lambda *_: (0,) * nd)


def _rms(x):
    return x * lax.rsqrt(jnp.mean(x * x, axis=-1, keepdims=True) + EPS)


def _headnorm64(h, g128, scale, o_ref):
    lo = lax.broadcasted_iota(jnp.int32, (1, LANES), 1) < A_HEAD_DIM
    for c in range(h.shape[1] // LANES):
        x = h[:, c * LANES:(c + 1) * LANES]
        x2 = x * x
        s_lo = jnp.sum(jnp.where(lo, x2, 0.0), axis=-1, keepdims=True)
        s_hi = jnp.sum(jnp.where(lo, 0.0, x2), axis=-1, keepdims=True)
        ms = jnp.where(lo, s_lo, s_hi) * (1.0 / A_HEAD_DIM)
        y = x * lax.rsqrt(ms + EPS) * g128
        if scale != 1.0:
            y = y * scale
        o_ref[c] = y.astype(o_ref.dtype)


def _in_proj_body(x_ref, nm_ref, wqa, wka, wva, wqb, wkb, wvb, wrb, wab, wqc, qna, kna, qnc, gup, gbias,
                  qa_o, ka_o, va_o, qb_o, kb_o, vb_o, rb_o, la_o, qc_o):
    xn = (_rms(x_ref[...]) * nm_ref[...]).astype(BF16)

    def mm(w):
        return jnp.dot(xn, w[...], preferred_element_type=F32)

    _headnorm64(mm(wqa), qna[...], A_HEAD_DIM ** -0.5, qa_o)
    _headnorm64(mm(wka), kna[...], 1.0, ka_o)
    hv = mm(wva)
    for c in range(A_WIDTH // LANES):
        va_o[c] = hv[:, c * LANES:(c + 1) * LANES]
    qb_o[...] = mm(wqb) * (B_DK ** -0.5)
    kb_o[...] = mm(wkb)
    vb_o[...] = mm(wvb)
    rb_o[...] = mm(wrb)
    z = jnp.dot(mm(wab), gup[...], precision=HIGHEST, preferred_element_type=F32) + gbias[...]
    la_o[...] = (jnp.minimum(z, 0.0) - jnp.log1p(jnp.exp(-jnp.abs(z)))) * (1.0 / B_TAU)
    hq = mm(wqc)
    for h in range(C_HEADS):
        sl = slice(h * C_HEAD_DIM, (h + 1) * C_HEAD_DIM)
        qc_o[:, sl] = _rms(hq[:, sl]) * qnc[...] * (C_HEAD_DIM ** -0.5)


def _in_proj(x2d, wts, tm):
    n = x2d.shape[0]
    ins = [x2d, wts['nm'], wts['wqa'], wts['wka'], wts['wva'], wts['wqb'], wts['wkb'], wts['wvb'], wts['wrb'],
           wts['wab'], wts['wqc'], wts['qna'], wts['kna'], wts['qnc'], wts['gup'], wts['gbias']]
    in_specs = [pl.BlockSpec((tm, D_MODEL), lambda i: (i, 0))] + [_const_spec(a.shape) for a in ins[1:]]
    nch = A_WIDTH // LANES
    widths = [B_KW, B_KW, B_VW, B_VW, B_KW, C_WIDTH]
    out_shape = ([jax.ShapeDtypeStruct((nch, n, LANES), F32)] * 3
                 + [jax.ShapeDtypeStruct((n, w), F32) for w in widths])
    out_specs = ([pl.BlockSpec((nch, tm, LANES), lambda i: (0, i, 0))] * 3
                 + [pl.BlockSpec((tm, w), lambda i: (i, 0)) for w in widths])
    return pl.pallas_call(
        _in_proj_body, grid=(n // tm,), in_specs=in_specs, out_specs=out_specs, out_shape=out_shape,
        compiler_params=_cparams(("parallel",)), name="in_proj")(*ins)


def _dil_prompt_body(q_ref, k_ref, v_ref, o_ref, os_ref, ls_ref, *, seq):
    g = pl.program_id(1)
    J = A_J
    row = lax.broadcasted_iota(jnp.int32, (J, J), 0)
    col = lax.broadcasted_iota(jnp.int32, (J, J), 1)
    cur_ok = col <= row
    prev_ok = col >= row

    for gi, (_, r) in enumerate(A_GROUPS):
        nb = seq // (r * J)

        @pl.when(g == gi)
        def _(gi=gi, r=r, nb=nb):
            def blk(i, carry):
                rho = i >> (nb.bit_length() - 1)
                n = i & (nb - 1)
                start = rho + n * (r * J)
                pstart = jnp.maximum(start - r * J, 0)
                if r == 1:
                    start, pstart = pl.multiple_of(start, J), pl.multiple_of(pstart, J)
                prev_pen = jnp.where(n > 0, 0.0, NEG).astype(F32)

                def ld(ref, c, s):
                    return ref[c, 0, pl.ds(s, J, stride=r), :].astype(BF16)

                for c in range(2):
                    q, kc, vc = ld(q_ref, c, start), ld(k_ref, c, start), ld(v_ref, c, start)
                    kp, vp = ld(k_ref, c, pstart), ld(v_ref, c, pstart)
                    o_parts, l_parts = [], []
                    for h in range(2):
                        sl = slice(h * A_HEAD_DIM, (h + 1) * A_HEAD_DIM)
                        s_c = lax.dot_general(q[:, sl], kc[:, sl], NT, preferred_element_type=F32)
                        s_p = lax.dot_general(q[:, sl], kp[:, sl], NT, preferred_element_type=F32)
                        s_c = jnp.where(cur_ok, s_c, NEG)
                        s_p = jnp.where(prev_ok, s_p, NEG) + prev_pen
                        m = jnp.maximum(jnp.max(s_c, axis=-1, keepdims=True), jnp.max(s_p, axis=-1, keepdims=True))
                        p_c = jnp.exp(s_c - m)
                        p_p = jnp.exp(s_p - m)
                        l = jnp.sum(p_c, axis=-1, keepdims=True) + jnp.sum(p_p, axis=-1, keepdims=True)
                        pv = (jnp.dot(p_c.astype(BF16), vc[:, sl], preferred_element_type=F32)
                              + jnp.dot(p_p.astype(BF16), vp[:, sl], preferred_element_type=F32))
                        o_parts.append(pv / l)
                        l_parts.append(jnp.broadcast_to(m + jnp.log(l), (J, A_HEAD_DIM)))
                    os_ref[gi, c, pl.ds(start, J, stride=r), :] = jnp.concatenate(o_parts, axis=-1)
                    ls_ref[gi, c, pl.ds(start, J, stride=r), :] = jnp.concatenate(l_parts, axis=-1)
                return carry

            lax.fori_loop(0, seq // J, blk, 0)

    @pl.when(g == len(A_GROUPS) - 1)
    def _():
        def comb(i, carry):
            sl = pl.ds(pl.multiple_of(i * J, J), J)
            for c in range(2):
                l0, l1, l2 = ls_ref[0, c, sl, :], ls_ref[1, c, sl, :], ls_ref[2, c, sl, :]
                mx = jnp.maximum(jnp.maximum(l0, l1), l2)
                w0, w1, w2 = jnp.exp(l0 - mx), jnp.exp(l1 - mx), jnp.exp(l2 - mx)
                num = w0 * os_ref[0, c, sl, :] + w1 * os_ref[1, c, sl, :] + w2 * os_ref[2, c, sl, :]
                o_ref[0, sl, c * LANES:(c + 1) * LANES] = (num / (w0 + w1 + w2)).astype(o_ref.dtype)
            return carry

        lax.fori_loop(0, seq // J, comb, 0)


def _dilated_prompt(qa, ka, va):
    _, bsz, seq, _ = qa.shape
    ng = len(A_GROUPS)
    spec = pl.BlockSpec((2, 1, seq, LANES), lambda b, g: (g, b, 0, 0))
    return pl.pallas_call(
        functools.partial(_dil_prompt_body, seq=seq), grid=(bsz, ng),
        in_specs=[spec, spec, spec],
        out_specs=pl.BlockSpec((1, seq, A_GW), lambda b, g: (b, 0, 0)),
        out_shape=jax.ShapeDtypeStruct((bsz, seq, A_GW), BF16),
        scratch_shapes=[pltpu.VMEM((ng, 2, seq, LANES), F32), pltpu.VMEM((ng, 2, seq, LANES), F32)],
        compiler_params=_cparams(("parallel", "arbitrary")), name="dilated_prompt")(qa, ka, va)


def _dil_sample_consts(n_new):
    ecat = np.zeros((n_new * A_GW, LANES), np.float32)
    for l in range(n_new):
        for h in range(A_HEADS):
            ecat[l * A_GW + h * A_HEAD_DIM: l * A_GW + (h + 1) * A_HEAD_DIM, A_HEADS * l + h] = 1.0
    return jnp.asarray(ecat, BF16), jnp.asarray(ecat.T, BF16)


def _dil_sample_body(q_ref, k_ref, v_ref, c1_ref, c2_ref, c3_ref, ecat_ref, etcat_ref,
                     o_ref, n1_ref, n2_ref, n3_ref, *, bb, n_new):
    L = n_new
    J = A_J
    caches = (c1_ref, c2_ref, c3_ref)
    outs = (n1_ref, n2_ref, n3_ref)
    ecat = ecat_ref[...]
    etcat = etcat_ref[...]
    hshift = A_HEADS.bit_length() - 1
    lane_q = lax.broadcasted_iota(jnp.int32, (J, LANES), 1) >> hshift
    row_c = lax.broadcasted_iota(jnp.int32, (J, LANES), 0)
    lane_qn = lax.broadcasted_iota(jnp.int32, (2 * L, LANES), 1) >> hshift
    row_n = lax.broadcasted_iota(jnp.int32, (2 * L, LANES), 0)

    NC = 2 * A_GW // LANES
    zeros_new = jnp.zeros((L, LANES), F32)
    for b in range(bb):
        s_cache, s_new, v_rows, v_news = [], [], [], []
        for gi, (w, r) in enumerate(A_GROUPS):
            cref = caches[gi]
            qg = [q_ref[2 * gi + c, b] for c in range(2)]
            kn = [k_ref[2 * gi + c, b] for c in range(2)]
            vn = [v_ref[2 * gi + c, b] for c in range(2)]
            outs[gi][b, 0:(w - L) * NC, :] = cref[b, L * NC:w * NC, :]
            for c, val in enumerate(kn + vn):
                outs[gi][b, pl.ds((w - L) * NC + c, L, stride=NC), :] = val
            kres, vres = {}, {}
            for rho in range(min(r, L)):
                kres[rho] = [cref[b, pl.ds(rho * NC + c, J, stride=r * NC), :] for c in range(2)]
                vres[rho] = [cref[b, pl.ds(rho * NC + 2 + c, J, stride=r * NC), :] for c in range(2)]
            kn16 = [jnp.concatenate([x, zeros_new], axis=0) for x in kn]
            prods, prods_n = [], []
            for l in range(L):
                for c in range(2):
                    ql = qg[c][l:l + 1, :]
                    prods.append((kres[l % r][c] * ql).astype(BF16))
                    prods_n.append((kn16[c] * ql).astype(BF16))
            sc = jnp.dot(jnp.concatenate(prods, axis=-1), ecat, preferred_element_type=F32)
            sn = jnp.dot(jnp.concatenate(prods_n, axis=-1), ecat, preferred_element_type=F32)
            sc = jnp.where(row_c >= (lane_q >> (r.bit_length() - 1)), sc, NEG)
            ok_n = (row_n <= lane_qn) & (((lane_qn - row_n) & (r - 1)) == 0) & (row_n < L)
            sn = jnp.where(ok_n, sn, NEG)
            s_cache.append(sc)
            s_new.append(sn)
            v_rows.append(vres)
            v_news.append([jnp.concatenate([x, zeros_new], axis=0) for x in vn])
        m = None
        for s in s_cache + s_new:
            ms = jnp.max(s, axis=0, keepdims=True)
            m = ms if m is None else jnp.maximum(m, ms)
        p_cache = [jnp.exp(s - m) for s in s_cache]
        p_new = [jnp.exp(s - m) for s in s_new]
        den = sum(jnp.sum(p, axis=0, keepdims=True) for p in p_cache + p_new)
        inv = 1.0 / den
        acc = [[jnp.zeros((1, LANES), F32) for _ in range(2)] for _ in range(L)]
        for gi, (w, r) in enumerate(A_GROUPS):
            pb = jnp.dot((p_cache[gi] * inv).astype(BF16), etcat, preferred_element_type=F32)
            pbn = jnp.dot((p_new[gi] * inv).astype(BF16), etcat, preferred_element_type=F32)
            for l in range(L):
                for c in range(2):
                    lsl = slice(l * A_GW + c * LANES, l * A_GW + (c + 1) * LANES)
                    acc[l][c] = (acc[l][c] + jnp.sum(pb[:, lsl] * v_rows[gi][l % r][c], axis=0, keepdims=True)
                                 + jnp.sum(pbn[:, lsl] * v_news[gi][c], axis=0, keepdims=True))
        for l in range(L):
            for c in range(2):
                o_ref[b, l:l + 1, c * LANES:(c + 1) * LANES] = acc[l][c].astype(o_ref.dtype)


def _dilated_sample(qa, ka, va, c1, c2, c3, bb):
    _, bd, n_new, _ = qa.shape
    ecat, etcat = _dil_sample_consts(n_new)
    new_spec = pl.BlockSpec((A_WIDTH // LANES, bb, n_new, LANES), lambda i: (0, i, 0, 0))
    cspecs = [pl.BlockSpec((bb,) + c.shape[1:], lambda i: (i, 0, 0)) for c in (c1, c2, c3)]
    return pl.pallas_call(
        functools.partial(_dil_sample_body, bb=bb, n_new=n_new), grid=(bd // bb,),
        in_specs=[new_spec, new_spec, new_spec] + cspecs + [_const_spec(ecat.shape), _const_spec(etcat.shape)],
        out_specs=[pl.BlockSpec((bb, n_new, A_GW), lambda i: (i, 0, 0))] + cspecs,
        out_shape=[jax.ShapeDtypeStruct((bd, n_new, A_GW), F32)]
        + [jax.ShapeDtypeStruct(c.shape, F32) for c in (c1, c2, c3)],
        compiler_params=_cparams(("parallel",)), name="dilated_sample")(qa, ka, va, c1, c2, c3, ecat, etcat)


def _gla_body(q_ref, k_ref, v_ref, la_ref, rb_ref, gn_ref, s0_ref, o_ref, sf_ref, s_ref, *, bb, tc, chunk):
    t = pl.program_id(1)
    C = chunk

    @pl.when(t == 0)
    def _():
        s_ref[...] = s0_ref[...]

    row = lax.broadcasted_iota(jnp.int32, (C, C), 0)
    col = lax.broadcasted_iota(jnp.int32, (C, C), 1)
    causal = row >= col
    tri = jnp.where(causal, 1.0, 0.0).astype(F32)
    eye_k = (lax.broadcasted_iota(jnp.int32, (B_DK, B_DK), 0) == lax.broadcasted_iota(jnp.int32, (B_DK, B_DK), 1))
    mid = C // 2 - 1
    gn = gn_ref[...]

    for b in range(bb):
        for c in range(tc // C):
            rows = slice(c * C, (c + 1) * C)
            la = la_ref[b, rows, :]
            cum = jnp.dot(tri, la, precision=HIGHEST, preferred_element_type=F32)
            cmid = cum[mid:mid + 1, :]
            clast = cum[C - 1:C, :]
            q = q_ref[b, rows, :]
            k = k_ref[b, rows, :]
            v = v_ref[b, rows, :].astype(BF16)
            q_in = (q * jnp.exp(cum)).astype(BF16)
            q_t = (q * jnp.exp(cum - cmid)).astype(BF16)
            k_t = (k * jnp.exp(cmid - cum)).astype(BF16)
            k_l = (k * jnp.exp(clast - cum)).astype(BF16)
            e_last = jnp.exp(clast)
            for h in range(B_HEADS):
                ks = slice(h * B_DK, (h + 1) * B_DK)
                vs = slice(h * B_DV, (h + 1) * B_DV)
                S = s_ref[b, h]
                o = jnp.dot(q_in[:, ks], S.astype(BF16), preferred_element_type=F32)
                att = lax.dot_general(q_t[:, ks], k_t[:, ks], NT, preferred_element_type=F32)
                att = jnp.where(causal, att, 0.0).astype(BF16)
                o = o + jnp.dot(att, v[:, vs], preferred_element_type=F32)
                e_col = jnp.sum(jnp.where(eye_k, e_last[:, ks], 0.0), axis=-1, keepdims=True)
                s_ref[b, h] = e_col * S + lax.dot_general(k_l[:, ks], v[:, vs], TN, preferred_element_type=F32)
                mu = jnp.mean(o, axis=-1, keepdims=True)
                d = o - mu
                var = jnp.mean(d * d, axis=-1, keepdims=True)
                rb = rb_ref[b, rows, vs]
                o_ref[b, rows, vs] = (d * lax.rsqrt(var + EPS) * gn * (rb * jax.nn.sigmoid(rb))).astype(o_ref.dtype)

    @pl.when(t == pl.num_programs(1) - 1)
    def _():
        sf_ref[...] = s_ref[...]


def _gla(qb, kb, vb, la, rb, gn, s0, bb, tc):
    bsz, seq, _ = qb.shape
    chunk = int(np.gcd(seq, B_CHUNK))
    kspec = pl.BlockSpec((bb, tc, B_KW), lambda i, t: (i, t, 0))
    vspec = pl.BlockSpec((bb, tc, B_VW), lambda i, t: (i, t, 0))
    sspec = pl.BlockSpec((bb, B_HEADS, B_DK, B_DV), lambda i, t: (i, 0, 0, 0))
    return pl.pallas_call(
        functools.partial(_gla_body, bb=bb, tc=tc, chunk=chunk), grid=(bsz // bb, seq // tc),
        in_specs=[kspec, kspec, vspec, kspec, vspec, _const_spec(gn.shape), sspec],
        out_specs=[vspec, sspec],
        out_shape=[jax.ShapeDtypeStruct((bsz, seq, B_VW), BF16), jax.ShapeDtypeStruct(s0.shape, F32)],
        scratch_shapes=[pltpu.VMEM((bb, B_HEADS, B_DK, B_DV), F32)],
        compiler_params=_cparams(("parallel", "arbitrary")), name="gla")(qb, kb, vb, la, rb, gn, s0)


def _mem_kv_body(m_ref, mn_ref, w_ref, knc_ref, o_ref):
    xn = (_rms(m_ref[...]) * mn_ref[...]).astype(BF16)
    kv = jnp.dot(xn, w_ref[...], preferred_element_type=F32)
    for h in range(C_HEADS):
        sl = slice(h * C_HEAD_DIM, (h + 1) * C_HEAD_DIM)
        o_ref[:, sl] = _rms(kv[:, sl]) * knc_ref[...]
    o_ref[:, C_WIDTH:] = kv[:, C_WIDTH:]


def _mem_kv(mem2d, mn, w, knc, tm):
    n = mem2d.shape[0]
    return pl.pallas_call(
        _mem_kv_body, grid=(n // tm,),
        in_specs=[pl.BlockSpec((tm, D_MODEL), lambda i: (i, 0)), _const_spec(mn.shape), _const_spec(w.shape),
                  _const_spec(knc.shape)],
        out_specs=pl.BlockSpec((tm, 2 * C_WIDTH), lambda i: (i, 0)),
        out_shape=jax.ShapeDtypeStruct((n, 2 * C_WIDTH), F32),
        compiler_params=_cparams(("parallel",)), name="mem_kv")(mem2d, mn, w, knc)


def _cross_body(q_ref, mem_ref, o_ref, *, bb):
    for b in range(bb):
        q = q_ref[b].astype(BF16)
        for h in range(C_HEADS):
            sl = slice(h * C_HEAD_DIM, (h + 1) * C_HEAD_DIM)
            kh = mem_ref[b, :, sl].astype(BF16)
            vh = mem_ref[b, :, C_WIDTH + h * C_HEAD_DIM:C_WIDTH + (h + 1) * C_HEAD_DIM].astype(BF16)
            s = lax.dot_general(q[:, sl], kh, NT, preferred_element_type=F32)
            p = jnp.exp(s - jnp.max(s, axis=-1, keepdims=True))
            l = jnp.sum(p, axis=-1, keepdims=True)
            o = jnp.dot(p.astype(BF16), vh, preferred_element_type=F32) / l
            o_ref[b, :, sl] = o.astype(o_ref.dtype)


def _cross(qc, mem_kv, bb, tq):
    bsz, seq, _ = qc.shape
    return pl.pallas_call(
        functools.partial(_cross_body, bb=bb), grid=(bsz // bb, seq // tq),
        in_specs=[pl.BlockSpec((bb, tq, C_WIDTH), lambda i, t: (i, t, 0)),
                  pl.BlockSpec((bb, N_MEM, 2 * C_WIDTH), lambda i, t: (i, 0, 0))],
        out_specs=pl.BlockSpec((bb, tq, C_WIDTH), lambda i, t: (i, t, 0)),
        out_shape=jax.ShapeDtypeStruct((bsz, seq, C_WIDTH), F32),
        compiler_params=_cparams(("parallel", "arbitrary")), name="cross")(qc, mem_kv)


def _merge_body(x_ref, nm_ref, wgl, oa_ref, ob_ref, oc_ref, wba, wbb, wbc, wout, nf_ref, wr, br,
                h_o, xt_o, route_o):
    x = x_ref[...]
    xn = (_rms(x) * nm_ref[...]).astype(BF16)
    merged = None
    for i, (o_ref, w_ref) in enumerate(((oa_ref, wba), (ob_ref, wbb), (oc_ref, wbc))):
        gate = jax.nn.sigmoid(jnp.dot(xn, wgl[:, i * D_MODEL:(i + 1) * D_MODEL], preferred_element_type=F32))
        term = gate * jnp.dot(o_ref[...].astype(BF16), w_ref[...], preferred_element_type=F32)
        merged = term if merged is None else merged + term
    h = x + jnp.dot(merged.astype(BF16), wout[...], preferred_element_type=F32)
    h_o[...] = h
    xt = _rms(h) * nf_ref[...]
    xt_o[...] = xt
    logits = jnp.dot(xt, wr[...], precision=HIGHEST, preferred_element_type=F32) + br[...]
    lane = lax.broadcasted_iota(jnp.int32, logits.shape, 1)
    lane_f = lane.astype(F32)
    big = float(LANES)
    is_g = (lane >= N_EXPERTS) & (lane < N_EXPERTS + N_GROUPS_E)
    glog = jnp.where(is_g, logits, NEG)
    gmax = jnp.max(glog, axis=-1, keepdims=True)
    grp = jnp.min(jnp.where(glog == gmax, lane_f, big), axis=-1, keepdims=True) - float(N_EXPERTS)
    g_w = 1.0 / jnp.sum(jnp.where(is_g, jnp.exp(glog - gmax), 0.0), axis=-1, keepdims=True)
    in_grp = (lane < N_EXPERTS) & (jnp.floor(lane_f * (1.0 / E_PER_GROUP)) == grp)
    elog = jnp.where(in_grp, logits, NEG)
    v1 = jnp.max(elog, axis=-1, keepdims=True)
    i1 = jnp.min(jnp.where(elog == v1, lane_f, big), axis=-1, keepdims=True)
    elog2 = jnp.where(lane_f == i1, NEG, elog)
    v2 = jnp.max(elog2, axis=-1, keepdims=True)
    i2 = jnp.min(jnp.where(elog2 == v2, lane_f, big), axis=-1, keepdims=True)
    e2 = jnp.exp(v2 - v1)
    w1 = g_w / (1.0 + e2)
    w2 = g_w * e2 / (1.0 + e2)
    route_o[...] = jnp.where(lane == 0, i1, jnp.where(lane == 1, i2, jnp.where(lane == 2, w1,
                             jnp.where(lane == 3, w2, 0.0))))


def _merge(x2d, oa, ob, oc, wts, tm):
    n = x2d.shape[0]
    consts = [wts['nm'], wts['wgl']]
    consts2 = [wts['wba'], wts['wbb'], wts['wbc'], wts['wout'], wts['nf'], wts['wr'], wts['br']]

    def tile(w):
        return pl.BlockSpec((tm, w), lambda i: (i, 0))

    in_specs = ([tile(D_MODEL)] + [_const_spec(a.shape) for a in consts]
                + [tile(A_GW), tile(B_VW), tile(C_WIDTH)] + [_const_spec(a.shape) for a in consts2])
    return pl.pallas_call(
        _merge_body, grid=(n // tm,), in_specs=in_specs,
        out_specs=[tile(D_MODEL), tile(D_MODEL), tile(LANES)],
        out_shape=[jax.ShapeDtypeStruct((n, D_MODEL), F32), jax.ShapeDtypeStruct((n, D_MODEL), F32),
                   jax.ShapeDtypeStruct((n, LANES), F32)],
        compiler_params=_cparams(("parallel",)), name="merge")(x2d, *consts, oa, ob, oc, *consts2)


def _gather_rows(idx_ref, n_rows, src_hbm, dst, sem):
    def body(i, carry):
        pltpu.make_async_copy(src_hbm.at[pl.ds(idx_ref[0, i], 1)], dst.at[pl.ds(i, 1)], sem).start()
        return carry
    lax.fori_loop(0, n_rows, body, 0)


def _wait_rows(n_rows, src_hbm, dst, sem):
    def body(i, carry):
        pltpu.make_async_copy(src_hbm.at[pl.ds(0, 1)], dst.at[pl.ds(i, 1)], sem).wait()
        return carry
    lax.fori_loop(0, n_rows, body, 0)


def _expert_body(be_ref, src_cur, src_next, xt_hbm, wgu_ref, wd_ref, y_ref, xbuf, sem, *, bm):
    del be_ref
    j = pl.program_id(0)
    nblk = pl.num_programs(0)
    slot = j % 2

    @pl.when(j == 0)
    def _():
        _gather_rows(src_cur, bm, xt_hbm, xbuf.at[0], sem.at[0])

    @pl.when(j + 1 < nblk)
    def _():
        _gather_rows(src_next, bm, xt_hbm, xbuf.at[1 - slot], sem.at[1 - slot])

    _wait_rows(bm, xt_hbm, xbuf.at[slot], sem.at[slot])
    xb = xbuf[slot].astype(BF16)
    gu = jnp.dot(xb, wgu_ref[0], preferred_element_type=F32)
    gate = gu[:, :D_EXPERT]
    hid = gate * jax.nn.sigmoid(gate) * gu[:, D_EXPERT:]
    y_ref[...] = jnp.dot(hid.astype(BF16), wd_ref[0], preferred_element_type=F32)


def _experts(xt, src, block_e, wgu, wd, bm):
    nblk = block_e.shape[0]
    src3 = src.reshape(nblk, 1, bm)
    grid_spec = pltpu.PrefetchScalarGridSpec(
        num_scalar_prefetch=1, grid=(nblk,),
        in_specs=[
            pl.BlockSpec((None, 1, bm), lambda j, be: (j, 0, 0), memory_space=pltpu.SMEM),
            pl.BlockSpec((None, 1, bm), lambda j, be: (jnp.minimum(j + 1, nblk - 1), 0, 0), memory_space=pltpu.SMEM),
            pl.BlockSpec(memory_space=pl.ANY),
            pl.BlockSpec((1, D_MODEL, 2 * D_EXPERT), lambda j, be: (be[j], 0, 0)),
            pl.BlockSpec((1, D_EXPERT, D_MODEL), lambda j, be: (be[j], 0, 0)),
        ],
        out_specs=pl.BlockSpec((bm, D_MODEL), lambda j, be: (j, 0)),
        scratch_shapes=[pltpu.VMEM((2, bm, D_MODEL), F32), pltpu.SemaphoreType.DMA((2,))])
    return pl.pallas_call(
        functools.partial(_expert_body, bm=bm), grid_spec=grid_spec,
        out_shape=jax.ShapeDtypeStruct((nblk * bm, D_MODEL), F32),
        compiler_params=_cparams(("arbitrary",)), name="experts")(block_e, src3, src3, xt, wgu, wd)


def _combine_body(d_cur, d_next, h_ref, route_ref, ys_hbm, y_ref, ybuf, sem, *, tm):
    j = pl.program_id(0)
    nblk = pl.num_programs(0)
    slot = j % 2

    @pl.when(j == 0)
    def _():
        _gather_rows(d_cur, 2 * tm, ys_hbm, ybuf.at[0], sem.at[0])

    @pl.when(j + 1 < nblk)
    def _():
        _gather_rows(d_next, 2 * tm, ys_hbm, ybuf.at[1 - slot], sem.at[1 - slot])

    _wait_rows(2 * tm, ys_hbm, ybuf.at[slot], sem.at[slot])
    route = route_ref[...]
    w1 = route[:, 2:3]
    w2 = route[:, 3:4]
    y_ref[...] = h_ref[...] + (ybuf[slot, 0:tm, :] * w1 + ybuf[slot, tm:2 * tm, :] * w2)


def _combine(h, route, ys, dest, tm):
    n = h.shape[0]
    nblk = n // tm
    d3 = dest.reshape(nblk, tm, 2).transpose(0, 2, 1).reshape(nblk, 1, 2 * tm)
    return pl.pallas_call(
        functools.partial(_combine_body, tm=tm), grid=(nblk,),
        in_specs=[
            pl.BlockSpec((None, 1, 2 * tm), lambda j: (j, 0, 0), memory_space=pltpu.SMEM),
            pl.BlockSpec((None, 1, 2 * tm), lambda j: (jnp.minimum(j + 1, nblk - 1), 0, 0), memory_space=pltpu.SMEM),
            pl.BlockSpec((tm, D_MODEL), lambda j: (j, 0)),
            pl.BlockSpec((tm, LANES), lambda j: (j, 0)),
            pl.BlockSpec(memory_space=pl.ANY),
        ],
        out_specs=pl.BlockSpec((tm, D_MODEL), lambda j: (j, 0)),
        out_shape=jax.ShapeDtypeStruct((n, D_MODEL), F32),
        scratch_shapes=[pltpu.VMEM((2, 2 * tm, D_MODEL), F32), pltpu.SemaphoreType.DMA((2,))],
        compiler_params=_cparams(("arbitrary",)), name="combine")(d3, d3, h, route, ys, )


def _dispatch_plan(route, bm):
    n = route.shape[0]
    flat_e = route[:, 0:2].astype(jnp.int32).reshape(-1)
    n_asg = flat_e.shape[0]
    onehot = (flat_e[:, None] == jnp.arange(N_EXPERTS, dtype=jnp.int32)[None, :]).astype(jnp.int32)
    csum = jnp.cumsum(onehot, axis=0)
    rank = jnp.take_along_axis(csum, flat_e[:, None], axis=1)[:, 0] - 1
    counts = csum[-1]
    padded = (counts + bm - 1) // bm * bm
    pad_end = jnp.cumsum(padded)
    dest = (pad_end - padded)[flat_e] + rank
    nblk = n_asg // bm + N_EXPERTS
    src = jnp.zeros((nblk * bm,), jnp.int32).at[dest].set(jnp.arange(n_asg, dtype=jnp.int32) // 2)
    block_e = jnp.minimum(jnp.searchsorted(pad_end, jnp.arange(nblk, dtype=jnp.int32) * bm, side='right'),
                          N_EXPERTS - 1).astype(jnp.int32)
    return dest.reshape(n, 2), src, block_e


def _hmoe(h, xt, route, wts, bm, tm):
    dest, src, block_e = _dispatch_plan(route, bm)
    ys = _experts(xt, src, block_e, wts['wgu'], wts['wd'], bm)
    return _combine(h, route, ys, dest, tm)


def _prep_weights(norm_mix, w_in, qn_a, kn_a, qn_c, kn_c, gla_gate_up, gla_gate_bias, gla_norm, mem_norm,
                  w_mem_kv, w_branch_a, w_branch_b, w_branch_c, w_out, norm_ffn, w_router_group, b_router_group,
                  w_router_expert, b_router_expert, w_exp_gate, w_exp_up, w_exp_down):
    splits = (A_WIDTH, A_WIDTH, A_WIDTH, B_KW, B_KW, B_VW, B_VW, B_RANK, C_WIDTH, 3 * D_MODEL)
    offs = np.concatenate([[0], np.cumsum(splits)])
    names = ('wqa', 'wka', 'wva', 'wqb', 'wkb', 'wvb', 'wrb', 'wab', 'wqc', 'wgl')
    wts = {nm: w_in[:, int(offs[i]):int(offs[i + 1])].astype(BF16) for i, nm in enumerate(names)}
    wts['wab'] = jnp.pad(wts['wab'], ((0, 0), (0, LANES - B_RANK)))
    wts['gup'] = jnp.pad(gla_gate_up.astype(F32), ((0, LANES - B_RANK), (0, 0)))
    wts['gbias'] = gla_gate_bias.reshape(1, B_KW)
    wts['nm'] = norm_mix.reshape(1, D_MODEL)
    wts['qna'] = jnp.tile(qn_a, 2).reshape(1, LANES)
    wts['kna'] = jnp.tile(kn_a, 2).reshape(1, LANES)
    wts['qnc'] = qn_c.reshape(1, C_HEAD_DIM)
    wts['knc'] = kn_c.reshape(1, C_HEAD_DIM)
    wts['gn'] = gla_norm.reshape(1, B_DV)
    wts['mn'] = mem_norm.reshape(1, D_MODEL)
    wts['wmem'] = w_mem_kv.astype(BF16)
    wts['wba'] = w_branch_a.astype(BF16)
    wts['wbb'] = w_branch_b.astype(BF16)
    wts['wbc'] = w_branch_c.astype(BF16)
    wts['wout'] = w_out.astype(BF16)
    wts['nf'] = norm_ffn.reshape(1, D_MODEL)
    wr = jnp.concatenate([w_router_expert, w_router_group], axis=1)
    br = jnp.concatenate([b_router_expert, b_router_group])
    npad = LANES - N_EXPERTS - N_GROUPS_E
    wts['wr'] = jnp.pad(wr, ((0, 0), (0, npad)))
    wts['br'] = jnp.pad(br, (0, npad)).reshape(1, LANES)
    wts['wgu'] = jnp.concatenate([w_exp_gate, w_exp_up], axis=2).astype(BF16)
    wts['wd'] = w_exp_down.astype(BF16)
    return wts


def _tile(n, want):
    t = min(n, want)
    assert n % t == 0
    return t


def _layer(x, wts, attend, gla_state, mem_kv, cfg):
    bsz, seq, _ = x.shape
    n = bsz * seq
    x2d = x.reshape(n, D_MODEL)
    tm = _tile(n, cfg['tm'])
    qa, ka, va, qb, kb, vb, rb, la, qc = _in_proj(x2d, wts, tm)

    def r3(a):
        return a.reshape(bsz, seq, a.shape[-1])

    def r4(a):
        return a.reshape(a.shape[0], bsz, seq, LANES)

    oa, extras = attend(r4(qa), r4(ka), r4(va))
    ob, s_fin = _gla(r3(qb), r3(kb), r3(vb), r3(la), r3(rb), wts['gn'], gla_state, cfg['gla_bb'],
                     _tile(seq, cfg['gla_tc']))
    oc = _cross(r3(qc), mem_kv, cfg['cross_bb'], _tile(seq, cfg['cross_tq']))
    h, xt, route = _merge(x2d, oa.reshape(n, A_GW), ob.reshape(n, B_VW), oc.reshape(n, C_WIDTH), wts, tm)
    y = _hmoe(h, xt, route, wts, cfg['moe_bm'], _tile(n, cfg['moe_tm']))
    return y.reshape(bsz, seq, D_MODEL), extras, s_fin


def kernel(x_prompt, x_sample, mem_prompt, cache_win1, cache_win2, cache_win3, state_gla, cache_mem, norm_mix, w_in, qn_a, kn_a, qn_c, kn_c, gla_gate_up, gla_gate_bias, gla_norm, mem_norm, w_mem_kv, w_branch_a, w_branch_b, w_branch_c, w_out, norm_ffn, w_router_group, b_router_group, w_router_expert, b_router_expert, w_exp_gate, w_exp_up, w_exp_down):
    wts = _prep_weights(norm_mix, w_in, qn_a, kn_a, qn_c, kn_c, gla_gate_up, gla_gate_bias, gla_norm, mem_norm,
                        w_mem_kv, w_branch_a, w_branch_b, w_branch_c, w_out, norm_ffn, w_router_group,
                        b_router_group, w_router_expert, b_router_expert, w_exp_gate, w_exp_up, w_exp_down)
    bsz, seq, _ = x_prompt.shape
    bd, n_new, _ = x_sample.shape

    mem_p = _mem_kv(mem_prompt.reshape(bsz * N_MEM, D_MODEL), wts['mn'], wts['wmem'], wts['knc'], 256)
    mem_p = mem_p.reshape(bsz, N_MEM, 2 * C_WIDTH)

    def attend_prompt(qa, ka, va):
        oa = _dilated_prompt(qa, ka, va)
        bufs = []
        for gi, (w, _) in enumerate(A_GROUPS):
            wb = min(w, seq)
            kv = jnp.concatenate([a[2 * gi + c, :, seq - wb:] for a in (ka, va) for c in range(2)], axis=-1)
            bufs.append(kv.reshape(bsz, wb, 2, A_HEADS, A_HEAD_DIM))
        return oa, bufs

    cfg_p = dict(tm=256, gla_bb=1, gla_tc=256, cross_bb=1, cross_tq=512, moe_bm=128, moe_tm=128)
    s0 = jnp.zeros((bsz, B_HEADS, B_DK, B_DV), F32)
    y_p, wins_p, gla_p = _layer(x_prompt, wts, attend_prompt, s0, mem_p, cfg_p)

    def attend_sample(qa, ka, va):
        caches = [c.reshape(bd, c.shape[1] * 2 * A_GW // LANES, LANES) for c in (cache_win1, cache_win2, cache_win3)]
        oa, n1, n2, n3 = _dilated_sample(qa, ka, va, *caches, bb=2)
        return oa, [n.reshape(c.shape) for n, c in zip((n1, n2, n3), (cache_win1, cache_win2, cache_win3))]

    cfg_s = dict(tm=256, gla_bb=8, gla_tc=n_new, cross_bb=8, cross_tq=n_new, moe_bm=128, moe_tm=128)
    y_s, wins_s, gla_s = _layer(x_sample, wts, attend_sample, state_gla,
                                cache_mem.reshape(bd, N_MEM, 2 * C_WIDTH), cfg_s)

    return (y_p, y_s, wins_p[0], wins_p[1], wins_p[2], gla_p, mem_p.reshape(bsz, N_MEM, 2, C_HEADS, C_HEAD_DIM),
            wins_s[0], wins_s[1], wins_s[2], gla_s)
```

```python
import functools

import jax
import jax.numpy as jnp
import numpy as np
from jax import lax
from jax.experimental import pallas as pl
from jax.experimental.pallas import tpu as pltpu

F32 = jnp.float32
BF16 = jnp.bfloat16
HIGHEST = lax.Precision.HIGHEST

D_MODEL = 1024
A_GROUPS = ((128, 1), (512, 4), (2048, 16))
A_J = 128
A_HEADS = 4
A_HEAD_DIM = 64
A_GW = A_HEADS * A_HEAD_DIM
A_WIDTH = 3 * A_GW
B_HEADS = 4
B_DK = 64
B_DV = 128
B_KW = B_HEADS * B_DK
B_VW = B_HEADS * B_DV
B_RANK = 16
B_TAU = 16.0
B_CHUNK = 64
C_HEADS = 4
C_HEAD_DIM = 128
C_WIDTH = C_HEADS * C_HEAD_DIM
N_MEM = 256
N_GROUPS_E = 4
E_PER_GROUP = 8
N_EXPERTS = 32
D_EXPERT = 256
EPS = 1e-6
NEG = -1e30
LANES = 128
VMEM_LIMIT = 56 * 1024 * 1024

NT = (((1,), (1,)), ((), ()))
TN = (((0,), (0,)), ((), ()))


def _cparams(sem):
    return pltpu.CompilerParams(dimension_semantics=sem, vmem_limit_bytes=VMEM_LIMIT)


def _const_spec(shape):
    nd = len(shape)
    return pl.BlockSpec(shape, lambda *_: (0,) * nd, pipeline_mode=pl.Buffered(1))


def _rms(x):
    return x * lax.rsqrt(jnp.mean(x * x, axis=-1, keepdims=True) + EPS)


def _headnorm64(h, g128, scale, o_ref):
    lo = lax.broadcasted_iota(jnp.int32, (1, LANES), 1) < A_HEAD_DIM
    for c in range(h.shape[1] // LANES):
        x = h[:, c * LANES:(c + 1) * LANES]
        x2 = x * x
        s_lo = jnp.sum(jnp.where(lo, x2, 0.0), axis=-1, keepdims=True)
        s_hi = jnp.sum(jnp.where(lo, 0.0, x2), axis=-1, keepdims=True)
        ms = jnp.where(lo, s_lo, s_hi) * (1.0 / A_HEAD_DIM)
        y = x * lax.rsqrt(ms + EPS) * g128
        if scale != 1.0:
            y = y * scale
        o_ref[c] = y.astype(o_ref.dtype)


def _in_proj_body(x_ref, nm_ref, wqa, wka, wva, wqb, wkb, wvb, wrb, wab, wqc, qna, kna, qnc, gup, gbias,
                  qa_o, ka_o, va_o, qb_o, kb_o, vb_o, rb_o, la_o, qc_o):
    xn = (_rms(x_ref[...]) * nm_ref[...]).astype(BF16)

    def mm(w):
        return jnp.dot(xn, w[...], preferred_element_type=F32)

    _headnorm64(mm(wqa), qna[...], A_HEAD_DIM ** -0.5, qa_o)
    _headnorm64(mm(wka), kna[...], 1.0, ka_o)
    hv = mm(wva)
    for c in range(A_WIDTH // LANES):
        va_o[c] = hv[:, c * LANES:(c + 1) * LANES]
    qb_o[...] = mm(wqb) * (B_DK ** -0.5)
    kb_o[...] = mm(wkb)
    vb_o[...] = mm(wvb)
    rb_o[...] = mm(wrb)
    z = jnp.dot(mm(wab), gup[...], precision=HIGHEST, preferred_element_type=F32) + gbias[...]
    la_o[...] = (jnp.minimum(z, 0.0) - jnp.log1p(jnp.exp(-jnp.abs(z)))) * (1.0 / B_TAU)
    hq = mm(wqc)
    for h in range(C_HEADS):
        sl = slice(h * C_HEAD_DIM, (h + 1) * C_HEAD_DIM)
        qc_o[:, sl] = _rms(hq[:, sl]) * qnc[...] * (C_HEAD_DIM ** -0.5)


def _in_proj(x2d, wts, tm):
    n = x2d.shape[0]
    ins = [x2d, wts['nm'], wts['wqa'], wts['wka'], wts['wva'], wts['wqb'], wts['wkb'], wts['wvb'], wts['wrb'],
           wts['wab'], wts['wqc'], wts['qna'], wts['kna'], wts['qnc'], wts['gup'], wts['gbias']]
    in_specs = [pl.BlockSpec((tm, D_MODEL), lambda i: (i, 0))] + [_const_spec(a.shape) for a in ins[1:]]
    nch = A_WIDTH // LANES
    widths = [B_KW, B_KW, B_VW, B_VW, B_KW, C_WIDTH]
    out_shape = ([jax.ShapeDtypeStruct((nch, n, LANES), F32)] * 3
                 + [jax.ShapeDtypeStruct((n, w), F32) for w in widths])
    out_specs = ([pl.BlockSpec((nch, tm, LANES), lambda i: (0, i, 0))] * 3
                 + [pl.BlockSpec((tm, w), lambda i: (i, 0)) for w in widths])
    return pl.pallas_call(
        _in_proj_body, grid=(n // tm,), in_specs=in_specs, out_specs=out_specs, out_shape=out_shape,
        compiler_params=_cparams(("parallel",)), name="in_proj")(*ins)


def _dil_prompt_body(q_ref, k_ref, v_ref, o_ref, os_ref, ls_ref, *, seq):
    g = pl.program_id(1)
    J = A_J
    row = lax.broadcasted_iota(jnp.int32, (J, J), 0)
    col = lax.broadcasted_iota(jnp.int32, (J, J), 1)
    cur_ok = col <= row
    prev_ok = col >= row
    lo = lax.broadcasted_iota(jnp.int32, (1, LANES), 1) < A_HEAD_DIM
    hi = jnp.logical_not(lo)

    for gi, (_, r) in enumerate(A_GROUPS):
        nb = seq // (r * J)

        @pl.when(g == gi)
        def _(gi=gi, r=r, nb=nb):
            def blk(i, carry):
                rho = i >> (nb.bit_length() - 1)
                n = i & (nb - 1)
                start = rho + n * (r * J)
                pstart = jnp.maximum(start - r * J, 0)
                if r == 1:
                    start, pstart = pl.multiple_of(start, J), pl.multiple_of(pstart, J)
                prev_pen = jnp.where(n > 0, 0.0, NEG).astype(F32)

                def ld(ref, c, s):
                    return ref[c, 0, pl.ds(s, J, stride=r), :]

                heads = []
                for c in range(2):
                    q = ld(q_ref, c, start)
                    kc, vc = ld(k_ref, c, start).astype(BF16), ld(v_ref, c, start).astype(BF16)
                    kp, vp = ld(k_ref, c, pstart).astype(BF16), ld(v_ref, c, pstart).astype(BF16)
                    for lanes_ok in (lo, hi):
                        qh = jnp.where(lanes_ok, q, 0.0).astype(BF16)
                        s_c = lax.dot_general(qh, kc, NT, preferred_element_type=F32)
                        s_p = lax.dot_general(qh, kp, NT, preferred_element_type=F32)
                        heads.append((s_c, s_p, vc, vp))
                outs = []
                for s_c, s_p, vc, vp in heads:
                    s_c = jnp.where(cur_ok, s_c, NEG)
                    s_p = jnp.where(prev_ok, s_p, NEG) + prev_pen
                    m = jnp.max(jnp.maximum(s_c, s_p), axis=-1, keepdims=True)
                    p_c = jnp.exp(s_c - m)
                    p_p = jnp.exp(s_p - m)
                    l = jnp.sum(p_c + p_p, axis=-1, keepdims=True)
                    pv = (jnp.dot(p_c.astype(BF16), vc, preferred_element_type=F32)
                          + jnp.dot(p_p.astype(BF16), vp, preferred_element_type=F32))
                    outs.append((pv / l, m + jnp.log(l)))
                for c in range(2):
                    (o0, l0), (o1, l1) = outs[2 * c], outs[2 * c + 1]
                    os_ref[gi, c, pl.ds(start, J, stride=r), :] = jnp.where(lo, o0, o1)
                    ls_ref[gi, c, pl.ds(start, J, stride=r), :] = jnp.where(lo, l0, l1)
                return carry

            lax.fori_loop(0, seq // J, blk, 0)

    @pl.when(g == len(A_GROUPS) - 1)
    def _():
        def comb(i, carry):
            sl = pl.ds(pl.multiple_of(i * J, J), J)
            for c in range(2):
                l0, l1, l2 = ls_ref[0, c, sl, :], ls_ref[1, c, sl, :], ls_ref[2, c, sl, :]
                mx = jnp.maximum(jnp.maximum(l0, l1), l2)
                w0, w1, w2 = jnp.exp(l0 - mx), jnp.exp(l1 - mx), jnp.exp(l2 - mx)
                num = w0 * os_ref[0, c, sl, :] + w1 * os_ref[1, c, sl, :] + w2 * os_ref[2, c, sl, :]
                o_ref[0, sl, c * LANES:(c + 1) * LANES] = (num / (w0 + w1 + w2)).astype(o_ref.dtype)
            return carry

        lax.fori_loop(0, seq // J, comb, 0)


def _dilated_prompt(qa, ka, va):
    _, bsz, seq, _ = qa.shape
    ng = len(A_GROUPS)
    spec = pl.BlockSpec((2, 1, seq, LANES), lambda b, g: (g, b, 0, 0))
    return pl.pallas_call(
        functools.partial(_dil_prompt_body, seq=seq), grid=(bsz, ng),
        in_specs=[spec, spec, spec],
        out_specs=pl.BlockSpec((1, seq, A_GW), lambda b, g: (b, 0, 0)),
        out_shape=jax.ShapeDtypeStruct((bsz, seq, A_GW), BF16),
        scratch_shapes=[pltpu.VMEM((ng, 2, seq, LANES), F32), pltpu.VMEM((ng, 2, seq, LANES), F32)],
        compiler_params=_cparams(("parallel", "arbitrary")), name="dilated_prompt")(qa, ka, va)


def _dil_sample_consts(n_new):
    ecat = np.zeros((n_new * A_GW, LANES), np.float32)
    for l in range(n_new):
        for h in range(A_HEADS):
            ecat[l * A_GW + h * A_HEAD_DIM: l * A_GW + (h + 1) * A_HEAD_DIM, A_HEADS * l + h] = 1.0
    return jnp.asarray(ecat, BF16), jnp.asarray(ecat.T, BF16)


def _dil_sample_body(q_ref, k_ref, v_ref, c1_ref, c2_ref, c3_ref, ecat_ref, etcat_ref,
                     o_ref, n1_ref, n2_ref, n3_ref, *, bb, n_new):
    L = n_new
    J = A_J
    caches = (c1_ref, c2_ref, c3_ref)
    outs = (n1_ref, n2_ref, n3_ref)
    ecat = ecat_ref[...]
    etcat = etcat_ref[...]
    hshift = A_HEADS.bit_length() - 1
    lane_q = lax.broadcasted_iota(jnp.int32, (J, LANES), 1) >> hshift
    row_c = lax.broadcasted_iota(jnp.int32, (J, LANES), 0)
    lane_qn = lax.broadcasted_iota(jnp.int32, (2 * L, LANES), 1) >> hshift
    row_n = lax.broadcasted_iota(jnp.int32, (2 * L, LANES), 0)

    NC = 2 * A_GW // LANES
    zeros_new = jnp.zeros((L, LANES), F32)
    for b in range(bb):
        s_cache, s_new, v_rows, v_news = [], [], [], []
        for gi, (w, r) in enumerate(A_GROUPS):
            cref = caches[gi]
            qg = [q_ref[2 * gi + c, b] for c in range(2)]
            kn = [k_ref[2 * gi + c, b] for c in range(2)]
            vn = [v_ref[2 * gi + c, b] for c in range(2)]
            outs[gi][b, 0:(w - L) * NC, :] = cref[b, L * NC:w * NC, :]
            for c, val in enumerate(kn + vn):
                outs[gi][b, pl.ds((w - L) * NC + c, L, stride=NC), :] = val
            kres, vres = {}, {}
            for rho in range(min(r, L)):
                kres[rho] = [cref[b, pl.ds(rho * NC + c, J, stride=r * NC), :] for c in range(2)]
                vres[rho] = [cref[b, pl.ds(rho * NC + 2 + c, J, stride=r * NC), :] for c in range(2)]
            kn16 = [jnp.concatenate([x, zeros_new], axis=0) for x in kn]
            prods, prods_n = [], []
            for l in range(L):
                for c in range(2):
                    ql = qg[c][l:l + 1, :]
                    prods.append((kres[l % r][c] * ql).astype(BF16))
                    prods_n.append((kn16[c] * ql).astype(BF16))
            sc = jnp.dot(jnp.concatenate(prods, axis=-1), ecat, preferred_element_type=F32)
            sn = jnp.dot(jnp.concatenate(prods_n, axis=-1), ecat, preferred_element_type=F32)
            sc = jnp.where(row_c >= (lane_q >> (r.bit_length() - 1)), sc, NEG)
            ok_n = (row_n <= lane_qn) & (((lane_qn - row_n) & (r - 1)) == 0) & (row_n < L)
            sn = jnp.where(ok_n, sn, NEG)
            s_cache.append(sc)
            s_new.append(sn)
            v_rows.append(vres)
            v_news.append([jnp.concatenate([x, zeros_new], axis=0) for x in vn])
        m = None
        for s in s_cache + s_new:
            ms = jnp.max(s, axis=0, keepdims=True)
            m = ms if m is None else jnp.maximum(m, ms)
        p_cache = [jnp.exp(s - m) for s in s_cache]
        p_new = [jnp.exp(s - m) for s in s_new]
        den = sum(jnp.sum(p, axis=0, keepdims=True) for p in p_cache + p_new)
        inv = 1.0 / den
        acc = [[jnp.zeros((1, LANES), F32) for _ in range(2)] for _ in range(L)]
        for gi, (w, r) in enumerate(A_GROUPS):
            pb = jnp.dot((p_cache[gi] * inv).astype(BF16), etcat, preferred_element_type=F32)
            pbn = jnp.dot((p_new[gi] * inv).astype(BF16), etcat, preferred_element_type=F32)
            for l in range(L):
                for c in range(2):
                    lsl = slice(l * A_GW + c * LANES, l * A_GW + (c + 1) * LANES)
                    acc[l][c] = (acc[l][c] + jnp.sum(pb[:, lsl] * v_rows[gi][l % r][c], axis=0, keepdims=True)
                                 + jnp.sum(pbn[:, lsl] * v_news[gi][c], axis=0, keepdims=True))
        for l in range(L):
            for c in range(2):
                o_ref[b, l:l + 1, c * LANES:(c + 1) * LANES] = acc[l][c].astype(o_ref.dtype)


def _dilated_sample(qa, ka, va, c1, c2, c3, bb):
    _, bd, n_new, _ = qa.shape
    ecat, etcat = _dil_sample_consts(n_new)
    new_spec = pl.BlockSpec((A_WIDTH // LANES, bb, n_new, LANES), lambda i: (0, i, 0, 0))
    cspecs = [pl.BlockSpec((bb,) + c.shape[1:], lambda i: (i, 0, 0)) for c in (c1, c2, c3)]
    return pl.pallas_call(
        functools.partial(_dil_sample_body, bb=bb, n_new=n_new), grid=(bd // bb,),
        in_specs=[new_spec, new_spec, new_spec] + cspecs + [_const_spec(ecat.shape), _const_spec(etcat.shape)],
        out_specs=[pl.BlockSpec((bb, n_new, A_GW), lambda i: (i, 0, 0))] + cspecs,
        out_shape=[jax.ShapeDtypeStruct((bd, n_new, A_GW), F32)]
        + [jax.ShapeDtypeStruct(c.shape, F32) for c in (c1, c2, c3)],
        compiler_params=_cparams(("parallel",)), name="dilated_sample")(qa, ka, va, c1, c2, c3, ecat, etcat)


def _gla_body(q_ref, k_ref, v_ref, la_ref, rb_ref, gn_ref, s0_ref, o_ref, sf_ref, s_ref, *, bb, tc, chunk):
    t = pl.program_id(1)
    C = chunk

    @pl.when(t == 0)
    def _():
        s_ref[...] = s0_ref[...]

    row = lax.broadcasted_iota(jnp.int32, (C, C), 0)
    col = lax.broadcasted_iota(jnp.int32, (C, C), 1)
    causal = row >= col
    tri = jnp.where(causal, 1.0, 0.0).astype(F32)
    eye_k = (lax.broadcasted_iota(jnp.int32, (B_DK, B_DK), 0) == lax.broadcasted_iota(jnp.int32, (B_DK, B_DK), 1))
    mid = C // 2 - 1
    gn = gn_ref[...]

    for b in range(bb):
        for c in range(tc // C):
            rows = slice(c * C, (c + 1) * C)
            la = la_ref[b, rows, :]
            cum = jnp.dot(tri, la, precision=HIGHEST, preferred_element_type=F32)
            cmid = cum[mid:mid + 1, :]
            clast = cum[C - 1:C, :]
            q = q_ref[b, rows, :]
            k = k_ref[b, rows, :]
            v = v_ref[b, rows, :].astype(BF16)
            q_in = (q * jnp.exp(cum)).astype(BF16)
            q_t = (q * jnp.exp(cum - cmid)).astype(BF16)
            k_t = (k * jnp.exp(cmid - cum)).astype(BF16)
            k_l = (k * jnp.exp(clast - cum)).astype(BF16)
            e_last = jnp.exp(clast)
            for h in range(B_HEADS):
                ks = slice(h * B_DK, (h + 1) * B_DK)
                vs = slice(h * B_DV, (h + 1) * B_DV)
                S = s_ref[b, h]
                o = jnp.dot(q_in[:, ks], S.astype(BF16), preferred_element_type=F32)
                att = lax.dot_general(q_t[:, ks], k_t[:, ks], NT, preferred_element_type=F32)
                att = jnp.where(causal, att, 0.0).astype(BF16)
                o = o + jnp.dot(att, v[:, vs], preferred_element_type=F32)
                e_col = jnp.sum(jnp.where(eye_k, e_last[:, ks], 0.0), axis=-1, keepdims=True)
                s_ref[b, h] = e_col * S + lax.dot_general(k_l[:, ks], v[:, vs], TN, preferred_element_type=F32)
                mu = jnp.mean(o, axis=-1, keepdims=True)
                d = o - mu
                var = jnp.mean(d * d, axis=-1, keepdims=True)
                rb = rb_ref[b, rows, vs]
                o_ref[b, rows, vs] = (d * lax.rsqrt(var + EPS) * gn * (rb * jax.nn.sigmoid(rb))).astype(o_ref.dtype)

    @pl.when(t == pl.num_programs(1) - 1)
    def _():
        sf_ref[...] = s_ref[...]


def _gla(qb, kb, vb, la, rb, gn, s0, bb, tc):
    bsz, seq, _ = qb.shape
    chunk = int(np.gcd(seq, B_CHUNK))
    kspec = pl.BlockSpec((bb, tc, B_KW), lambda i, t: (i, t, 0))
    vspec = pl.BlockSpec((bb, tc, B_VW), lambda i, t: (i, t, 0))
    sspec = pl.BlockSpec((bb, B_HEADS, B_DK, B_DV), lambda i, t: (i, 0, 0, 0))
    return pl.pallas_call(
        functools.partial(_gla_body, bb=bb, tc=tc, chunk=chunk), grid=(bsz // bb, seq // tc),
        in_specs=[kspec, kspec, vspec, kspec, vspec, _const_spec(gn.shape), sspec],
        out_specs=[vspec, sspec],
        out_shape=[jax.ShapeDtypeStruct((bsz, seq, B_VW), BF16), jax.ShapeDtypeStruct(s0.shape, F32)],
        scratch_shapes=[pltpu.VMEM((bb, B_HEADS, B_DK, B_DV), F32)],
        compiler_params=_cparams(("parallel", "arbitrary")), name="gla")(qb, kb, vb, la, rb, gn, s0)


def _mem_kv_body(m_ref, mn_ref, w_ref, knc_ref, o_ref):
    xn = (_rms(m_ref[...]) * mn_ref[...]).astype(BF16)
    kv = jnp.dot(xn, w_ref[...], preferred_element_type=F32)
    for h in range(C_HEADS):
        sl = slice(h * C_HEAD_DIM, (h + 1) * C_HEAD_DIM)
        o_ref[:, sl] = _rms(kv[:, sl]) * knc_ref[...]
    o_ref[:, C_WIDTH:] = kv[:, C_WIDTH:]


def _mem_kv(mem2d, mn, w, knc, tm):
    n = mem2d.shape[0]
    return pl.pallas_call(
        _mem_kv_body, grid=(n // tm,),
        in_specs=[pl.BlockSpec((tm, D_MODEL), lambda i: (i, 0)), _const_spec(mn.shape), _const_spec(w.shape),
                  _const_spec(knc.shape)],
        out_specs=pl.BlockSpec((tm, 2 * C_WIDTH), lambda i: (i, 0)),
        out_shape=jax.ShapeDtypeStruct((n, 2 * C_WIDTH), F32),
        compiler_params=_cparams(("parallel",)), name="mem_kv")(mem2d, mn, w, knc)


def _cross_body(q_ref, mem_ref, o_ref, *, bb, head_rows):
    for b in range(bb):
        q = q_ref[b].astype(BF16)
        for h in range(C_HEADS):
            sl = slice(h * C_HEAD_DIM, (h + 1) * C_HEAD_DIM)
            if head_rows:
                kh = mem_ref[b, pl.ds(h, N_MEM, stride=2 * C_HEADS), :].astype(BF16)
                vh = mem_ref[b, pl.ds(C_HEADS + h, N_MEM, stride=2 * C_HEADS), :].astype(BF16)
            else:
                kh = mem_ref[b, :, sl].astype(BF16)
                vh = mem_ref[b, :, C_WIDTH + h * C_HEAD_DIM:C_WIDTH + (h + 1) * C_HEAD_DIM].astype(BF16)
            s = lax.dot_general(q[:, sl], kh, NT, preferred_element_type=F32)
            p = jnp.exp(s - jnp.max(s, axis=-1, keepdims=True))
            l = jnp.sum(p, axis=-1, keepdims=True)
            o = jnp.dot(p.astype(BF16), vh, preferred_element_type=F32) / l
            o_ref[b, :, sl] = o.astype(o_ref.dtype)


def _cross(qc, mem_kv, bb, tq):
    bsz, seq, _ = qc.shape
    head_rows = mem_kv.shape[-1] == C_HEAD_DIM
    return pl.pallas_call(
        functools.partial(_cross_body, bb=bb, head_rows=head_rows), grid=(bsz // bb, seq // tq),
        in_specs=[pl.BlockSpec((bb, tq, C_WIDTH), lambda i, t: (i, t, 0)),
                  pl.BlockSpec((bb,) + mem_kv.shape[1:], lambda i, t: (i, 0, 0))],
        out_specs=pl.BlockSpec((bb, tq, C_WIDTH), lambda i, t: (i, t, 0)),
        out_shape=jax.ShapeDtypeStruct((bsz, seq, C_WIDTH), F32),
        compiler_params=_cparams(("parallel", "arbitrary")), name="cross")(qc, mem_kv)


def _merge_body(x_ref, nm_ref, wgl, oa_ref, ob_ref, oc_ref, wba, wbb, wbc, wout, nf_ref, wr, br,
                h_o, xt_o, route_o):
    x = x_ref[...]
    xn = (_rms(x) * nm_ref[...]).astype(BF16)
    merged = None
    for i, (o_ref, w_ref) in enumerate(((oa_ref, wba), (ob_ref, wbb), (oc_ref, wbc))):
        gate = jax.nn.sigmoid(jnp.dot(xn, wgl[:, i * D_MODEL:(i + 1) * D_MODEL], preferred_element_type=F32))
        term = gate * jnp.dot(o_ref[...].astype(BF16), w_ref[...], preferred_element_type=F32)
        merged = term if merged is None else merged + term
    h = x + jnp.dot(merged.astype(BF16), wout[...], preferred_element_type=F32)
    h_o[...] = h
    xt = _rms(h) * nf_ref[...]
    xt_o[...] = xt
    xt_hi = xt.astype(BF16)
    xt_lo = (xt - xt_hi.astype(F32)).astype(BF16)
    logits = (jnp.dot(xt_hi, wr[0], preferred_element_type=F32) + jnp.dot(xt_hi, wr[1], preferred_element_type=F32)
              + jnp.dot(xt_lo, wr[0], preferred_element_type=F32) + br[...])
    lane = lax.broadcasted_iota(jnp.int32, logits.shape, 1)
    lane_f = lane.astype(F32)
    big = float(LANES)
    is_g = (lane >= N_EXPERTS) & (lane < N_EXPERTS + N_GROUPS_E)
    glog = jnp.where(is_g, logits, NEG)
    gmax = jnp.max(glog, axis=-1, keepdims=True)
    grp = jnp.min(jnp.where(glog == gmax, lane_f, big), axis=-1, keepdims=True) - float(N_EXPERTS)
    g_w = 1.0 / jnp.sum(jnp.where(is_g, jnp.exp(glog - gmax), 0.0), axis=-1, keepdims=True)
    in_grp = (lane < N_EXPERTS) & (jnp.floor(lane_f * (1.0 / E_PER_GROUP)) == grp)
    elog = jnp.where(in_grp, logits, NEG)
    v1 = jnp.max(elog, axis=-1, keepdims=True)
    i1 = jnp.min(jnp.where(elog == v1, lane_f, big), axis=-1, keepdims=True)
    elog2 = jnp.where(lane_f == i1, NEG, elog)
    v2 = jnp.max(elog2, axis=-1, keepdims=True)
    i2 = jnp.min(jnp.where(elog2 == v2, lane_f, big), axis=-1, keepdims=True)
    e2 = jnp.exp(v2 - v1)
    w1 = g_w / (1.0 + e2)
    w2 = g_w * e2 / (1.0 + e2)
    route_o[...] = jnp.where(lane == 0, i1, jnp.where(lane == 1, i2, jnp.where(lane == 2, w1,
                             jnp.where(lane == 3, w2, 0.0))))


def _merge(x2d, oa, ob, oc, wts, tm):
    n = x2d.shape[0]
    consts = [wts['nm'], wts['wgl']]
    consts2 = [wts['wba'], wts['wbb'], wts['wbc'], wts['wout'], wts['nf'], wts['wr'], wts['br']]

    def tile(w):
        return pl.BlockSpec((tm, w), lambda i: (i, 0))

    in_specs = ([tile(D_MODEL)] + [_const_spec(a.shape) for a in consts]
                + [tile(A_GW), tile(B_VW), tile(C_WIDTH)] + [_const_spec(a.shape) for a in consts2])
    return pl.pallas_call(
        _merge_body, grid=(n // tm,), in_specs=in_specs,
        out_specs=[tile(D_MODEL), tile(D_MODEL), tile(LANES)],
        out_shape=[jax.ShapeDtypeStruct((n, D_MODEL), F32), jax.ShapeDtypeStruct((n, D_MODEL), F32),
                   jax.ShapeDtypeStruct((n, LANES), F32)],
        compiler_params=_cparams(("parallel",)), name="merge")(x2d, *consts, oa, ob, oc, *consts2)


ROW_DMA_UNROLL = 8


def _dispatch_body(dest_ref, xt_ref, xs_in, xs_out, stage, sem, *, tm):
    del xs_in
    j = pl.program_id(0)
    nblk = pl.num_programs(0)
    slot = j % 2

    def wait_slot(s):
        for _ in range(2):
            pltpu.make_async_copy(stage.at[s], xs_out.at[pl.ds(0, tm)], sem.at[s]).wait()

    @pl.when(j >= 2)
    def _():
        wait_slot(slot)

    stage[slot] = xt_ref[...]
    for k in range(2):
        def body(i, carry, k=k):
            pltpu.make_async_copy(stage.at[slot, pl.ds(i, 1)], xs_out.at[pl.ds(dest_ref[0, k * tm + i], 1)],
                                  sem.at[slot]).start()
            return carry
        lax.fori_loop(0, tm, body, 0, unroll=ROW_DMA_UNROLL)

    @pl.when(j == nblk - 1)
    def _():
        wait_slot(slot)

        @pl.when(j >= 1)
        def _():
            wait_slot(1 - slot)


def _dispatch(xt, dest3, n_slots, tm):
    n = xt.shape[0]
    zeros = jnp.zeros((n_slots, D_MODEL), F32)
    return pl.pallas_call(
        functools.partial(_dispatch_body, tm=tm), grid=(n // tm,),
        in_specs=[pl.BlockSpec((None, 1, 2 * tm), lambda j: (j, 0, 0), memory_space=pltpu.SMEM),
                  pl.BlockSpec((tm, D_MODEL), lambda j: (j, 0)),
                  pl.BlockSpec(memory_space=pl.ANY)],
        out_specs=pl.BlockSpec(memory_space=pl.ANY),
        out_shape=jax.ShapeDtypeStruct((n_slots, D_MODEL), F32),
        scratch_shapes=[pltpu.VMEM((2, tm, D_MODEL), F32), pltpu.SemaphoreType.DMA((2,))],
        input_output_aliases={2: 0},
        compiler_params=_cparams(("arbitrary",)), name="dispatch")(dest3, xt, zeros)


def _expert_body(be_ref, xs_ref, wgu_ref, wd_ref, y_ref):
    del be_ref
    gu = jnp.dot(xs_ref[...].astype(BF16), wgu_ref[0], preferred_element_type=F32)
    gate = gu[:, :D_EXPERT]
    hid = gate * jax.nn.sigmoid(gate) * gu[:, D_EXPERT:]
    y_ref[...] = jnp.dot(hid.astype(BF16), wd_ref[0], preferred_element_type=F32)


def _experts(xs, block_e, wgu, wd, bm):
    nblk = block_e.shape[0]
    grid_spec = pltpu.PrefetchScalarGridSpec(
        num_scalar_prefetch=1, grid=(nblk,),
        in_specs=[
            pl.BlockSpec((bm, D_MODEL), lambda j, be: (j, 0)),
            pl.BlockSpec((1, D_MODEL, 2 * D_EXPERT), lambda j, be: (be[j], 0, 0)),
            pl.BlockSpec((1, D_EXPERT, D_MODEL), lambda j, be: (be[j], 0, 0)),
        ],
        out_specs=pl.BlockSpec((bm, D_MODEL), lambda j, be: (j, 0)))
    return pl.pallas_call(
        _expert_body, grid_spec=grid_spec,
        out_shape=jax.ShapeDtypeStruct((nblk * bm, D_MODEL), F32),
        compiler_params=_cparams(("parallel",)), name="experts")(block_e, xs, wgu, wd)


def _combine_body(d_cur, d_next, h_ref, route_ref, ys_hbm, y_ref, ybuf, sem, *, tm):
    j = pl.program_id(0)
    nblk = pl.num_programs(0)
    slot = j % 2

    def gather(idx_ref, s):
        def body(i, carry):
            pltpu.make_async_copy(ys_hbm.at[pl.ds(idx_ref[0, i], 1)], ybuf.at[s, pl.ds(i, 1)], sem.at[s]).start()
            return carry
        lax.fori_loop(0, 2 * tm, body, 0, unroll=ROW_DMA_UNROLL)

    @pl.when(j == 0)
    def _():
        gather(d_cur, 0)

    @pl.when(j + 1 < nblk)
    def _():
        gather(d_next, 1 - slot)

    pltpu.make_async_copy(ys_hbm.at[pl.ds(0, 2 * tm)], ybuf.at[slot], sem.at[slot]).wait()
    route = route_ref[...]
    w1 = route[:, 2:3]
    w2 = route[:, 3:4]
    y_ref[...] = h_ref[...] + (ybuf[slot, 0:tm, :] * w1 + ybuf[slot, tm:2 * tm, :] * w2)


def _tile_slots(dest, tm):
    nblk = dest.shape[0] // tm
    return dest.reshape(nblk, tm, 2).transpose(0, 2, 1).reshape(nblk, 1, 2 * tm)


def _combine(h, route, ys, dest, tm):
    n = h.shape[0]
    nblk = n // tm
    d3 = _tile_slots(dest, tm)
    return pl.pallas_call(
        functools.partial(_combine_body, tm=tm), grid=(nblk,),
        in_specs=[
            pl.BlockSpec((None, 1, 2 * tm), lambda j: (j, 0, 0), memory_space=pltpu.SMEM),
            pl.BlockSpec((None, 1, 2 * tm), lambda j: (jnp.minimum(j + 1, nblk - 1), 0, 0), memory_space=pltpu.SMEM),
            pl.BlockSpec((tm, D_MODEL), lambda j: (j, 0)),
            pl.BlockSpec((tm, LANES), lambda j: (j, 0)),
            pl.BlockSpec(memory_space=pl.ANY),
        ],
        out_specs=pl.BlockSpec((tm, D_MODEL), lambda j: (j, 0)),
        out_shape=jax.ShapeDtypeStruct((n, D_MODEL), F32),
        scratch_shapes=[pltpu.VMEM((2, 2 * tm, D_MODEL), F32), pltpu.SemaphoreType.DMA((2,))],
        compiler_params=_cparams(("arbitrary",)), name="combine")(d3, d3, h, route, ys, )


def _dispatch_plan(route, bm):
    n = route.shape[0]
    flat_e = route[:, 0:2].astype(jnp.int32).reshape(-1)
    n_asg = flat_e.shape[0]
    onehot = (flat_e[:, None] == jnp.arange(N_EXPERTS, dtype=jnp.int32)[None, :]).astype(jnp.int32)
    csum = jnp.cumsum(onehot, axis=0)
    rank = jnp.sum(csum * onehot, axis=1) - 1
    counts = csum[-1]
    padded = (counts + bm - 1) // bm * bm
    pad_end = jnp.cumsum(padded)
    dest = jnp.sum((pad_end - padded)[None, :] * onehot, axis=1) + rank
    nblk = n_asg // bm + N_EXPERTS
    first_slot = jnp.arange(nblk, dtype=jnp.int32) * bm
    block_e = jnp.minimum(jnp.sum((pad_end[None, :] <= first_slot[:, None]).astype(jnp.int32), axis=1),
                          N_EXPERTS - 1)
    return dest.reshape(n, 2), block_e


def _hmoe(h, xt, route, wts, bm, tm):
    dest, block_e = _dispatch_plan(route, bm)
    xs = _dispatch(xt, _tile_slots(dest, tm), block_e.shape[0] * bm, tm)
    ys = _experts(xs, block_e, wts['wgu'], wts['wd'], bm)
    return _combine(h, route, ys, dest, tm)


def _prep_weights(norm_mix, w_in, qn_a, kn_a, qn_c, kn_c, gla_gate_up, gla_gate_bias, gla_norm, mem_norm,
                  w_mem_kv, w_branch_a, w_branch_b, w_branch_c, w_out, norm_ffn, w_router_group, b_router_group,
                  w_router_expert, b_router_expert, w_exp_gate, w_exp_up, w_exp_down):
    splits = (A_WIDTH, A_WIDTH, A_WIDTH, B_KW, B_KW, B_VW, B_VW, B_RANK, C_WIDTH, 3 * D_MODEL)
    offs = np.concatenate([[0], np.cumsum(splits)])
    names = ('wqa', 'wka', 'wva', 'wqb', 'wkb', 'wvb', 'wrb', 'wab', 'wqc', 'wgl')
    wts = {nm: w_in[:, int(offs[i]):int(offs[i + 1])].astype(BF16) for i, nm in enumerate(names)}
    wts['wab'] = jnp.pad(wts['wab'], ((0, 0), (0, LANES - B_RANK)))
    wts['gup'] = jnp.pad(gla_gate_up.astype(F32), ((0, LANES - B_RANK), (0, 0)))
    wts['gbias'] = gla_gate_bias.reshape(1, B_KW)
    wts['nm'] = norm_mix.reshape(1, D_MODEL)
    wts['qna'] = jnp.tile(qn_a, 2).reshape(1, LANES)
    wts['kna'] = jnp.tile(kn_a, 2).reshape(1, LANES)
    wts['qnc'] = qn_c.reshape(1, C_HEAD_DIM)
    wts['knc'] = kn_c.reshape(1, C_HEAD_DIM)
    wts['gn'] = gla_norm.reshape(1, B_DV)
    wts['mn'] = mem_norm.reshape(1, D_MODEL)
    wts['wmem'] = w_mem_kv.astype(BF16)
    wts['wba'] = w_branch_a.astype(BF16)
    wts['wbb'] = w_branch_b.astype(BF16)
    wts['wbc'] = w_branch_c.astype(BF16)
    wts['wout'] = w_out.astype(BF16)
    wts['nf'] = norm_ffn.reshape(1, D_MODEL)
    wr = jnp.concatenate([w_router_expert, w_router_group], axis=1)
    br = jnp.concatenate([b_router_expert, b_router_group])
    npad = LANES - N_EXPERTS - N_GROUPS_E
    wr = jnp.pad(wr, ((0, 0), (0, npad)))
    wr_hi = wr.astype(BF16)
    wts['wr'] = jnp.stack([wr_hi, (wr - wr_hi.astype(F32)).astype(BF16)])
    wts['br'] = jnp.pad(br, (0, npad)).reshape(1, LANES)
    wts['wgu'] = jnp.concatenate([w_exp_gate, w_exp_up], axis=2).astype(BF16)
    wts['wd'] = w_exp_down.astype(BF16)
    return wts


def _tile(n, want):
    t = min(n, want)
    assert n % t == 0
    return t


def _layer(x, wts, attend, gla_state, mem_kv, cfg):
    bsz, seq, _ = x.shape
    n = bsz * seq
    x2d = x.reshape(n, D_MODEL)
    tm = _tile(n, cfg['tm'])
    qa, ka, va, qb, kb, vb, rb, la, qc = _in_proj(x2d, wts, tm)

    def r3(a):
        return a.reshape(bsz, seq, a.shape[-1])

    def r4(a):
        return a.reshape(a.shape[0], bsz, seq, LANES)

    oa, extras = attend(r4(qa), r4(ka), r4(va))
    ob, s_fin = _gla(r3(qb), r3(kb), r3(vb), r3(la), r3(rb), wts['gn'], gla_state, cfg['gla_bb'],
                     _tile(seq, cfg['gla_tc']))
    oc = _cross(r3(qc), mem_kv, cfg['cross_bb'], _tile(seq, cfg['cross_tq']))
    h, xt, route = _merge(x2d, oa.reshape(n, A_GW), ob.reshape(n, B_VW), oc.reshape(n, C_WIDTH), wts, tm)
    y = _hmoe(h, xt, route, wts, cfg['moe_bm'], _tile(n, cfg['moe_tm']))
    return y.reshape(bsz, seq, D_MODEL), extras, s_fin


def kernel(x_prompt, x_sample, mem_prompt, cache_win1, cache_win2, cache_win3, state_gla, cache_mem, norm_mix, w_in, qn_a, kn_a, qn_c, kn_c, gla_gate_up, gla_gate_bias, gla_norm, mem_norm, w_mem_kv, w_branch_a, w_branch_b, w_branch_c, w_out, norm_ffn, w_router_group, b_router_group, w_router_expert, b_router_expert, w_exp_gate, w_exp_up, w_exp_down):
    wts = _prep_weights(norm_mix, w_in, qn_a, kn_a, qn_c, kn_c, gla_gate_up, gla_gate_bias, gla_norm, mem_norm,
                        w_mem_kv, w_branch_a, w_branch_b, w_branch_c, w_out, norm_ffn, w_router_group,
                        b_router_group, w_router_expert, b_router_expert, w_exp_gate, w_exp_up, w_exp_down)
    bsz, seq, _ = x_prompt.shape
    bd, n_new, _ = x_sample.shape

    mem_p = _mem_kv(mem_prompt.reshape(bsz * N_MEM, D_MODEL), wts['mn'], wts['wmem'], wts['knc'], 256)
    mem_p = mem_p.reshape(bsz, N_MEM, 2 * C_WIDTH)

    def attend_prompt(qa, ka, va):
        oa = _dilated_prompt(qa, ka, va)
        bufs = []
        for gi, (w, _) in enumerate(A_GROUPS):
            wb = min(w, seq)
            kv = jnp.concatenate([a[2 * gi + c, :, seq - wb:] for a in (ka, va) for c in range(2)], axis=-1)
            bufs.append(kv.reshape(bsz, wb, 2, A_HEADS, A_HEAD_DIM))
        return oa, bufs

    cfg_p = dict(tm=512, gla_bb=1, gla_tc=256, cross_bb=1, cross_tq=512, moe_bm=256, moe_tm=256)
    s0 = jnp.zeros((bsz, B_HEADS, B_DK, B_DV), F32)
    y_p, wins_p, gla_p = _layer(x_prompt, wts, attend_prompt, s0, mem_p, cfg_p)

    def attend_sample(qa, ka, va):
        caches = [c.reshape(bd, c.shape[1] * 2 * A_GW // LANES, LANES) for c in (cache_win1, cache_win2, cache_win3)]
        oa, n1, n2, n3 = _dilated_sample(qa, ka, va, *caches, bb=2)
        return oa, [n.reshape(c.shape) for n, c in zip((n1, n2, n3), (cache_win1, cache_win2, cache_win3))]

    cfg_s = dict(tm=256, gla_bb=8, gla_tc=n_new, cross_bb=8, cross_tq=n_new, moe_bm=128, moe_tm=128)
    y_s, wins_s, gla_s = _layer(x_sample, wts, attend_sample, state_gla,
                                cache_mem.reshape(bd, N_MEM * 2 * C_HEADS, C_HEAD_DIM), cfg_s)

    return (y_p, y_s, wins_p[0], wins_p[1], wins_p[2], gla_p, mem_p.reshape(bsz, N_MEM, 2, C_HEADS, C_HEAD_DIM),
            wins_s[0], wins_s[1], wins_s[2], gla_s)
```

```python
import functools

import jax
import jax.numpy as jnp
import numpy as np
from jax import lax
from jax.experimental import pallas as pl
from jax.experimental.pallas import tpu as pltpu

F32 = jnp.float32
BF16 = jnp.bfloat16
HIGHEST = lax.Precision.HIGHEST

D_MODEL = 1024
A_GROUPS = ((128, 1), (512, 4), (2048, 16))
A_J = 128
A_HEADS = 4
A_HEAD_DIM = 64
A_GW = A_HEADS * A_HEAD_DIM
A_WIDTH = 3 * A_GW
B_HEADS = 4
B_DK = 64
B_DV = 128
B_KW = B_HEADS * B_DK
B_VW = B_HEADS * B_DV
B_RANK = 16
B_TAU = 16.0
B_CHUNK = 64
C_HEADS = 4
C_HEAD_DIM = 128
C_WIDTH = C_HEADS * C_HEAD_DIM
N_MEM = 256
N_GROUPS_E = 4
E_PER_GROUP = 8
N_EXPERTS = 32
D_EXPERT = 256
EPS = 1e-6
NEG = -1e30
LANES = 128
VMEM_LIMIT = 56 * 1024 * 1024

NT = (((1,), (1,)), ((), ()))
TN = (((0,), (0,)), ((), ()))


def _cparams(sem):
    return pltpu.CompilerParams(dimension_semantics=sem, vmem_limit_bytes=VMEM_LIMIT)


def _const_spec(shape, single=True):
    nd = len(shape)
    if single:
        return pl.BlockSpec(shape, lambda *_: (0,) * nd, pipeline_mode=pl.Buffered(1))
    return pl.BlockSpec(shape, lambda *_: (0,) * nd)


def _rms(x):
    return x * lax.rsqrt(jnp.mean(x * x, axis=-1, keepdims=True) + EPS)


def _headnorm64(h, g128, scale, o_ref):
    lo = lax.broadcasted_iota(jnp.int32, (1, LANES), 1) < A_HEAD_DIM
    for c in range(h.shape[1] // LANES):
        x = h[:, c * LANES:(c + 1) * LANES]
        x2 = x * x
        s_lo = jnp.sum(jnp.where(lo, x2, 0.0), axis=-1, keepdims=True)
        s_hi = jnp.sum(jnp.where(lo, 0.0, x2), axis=-1, keepdims=True)
        ms = jnp.where(lo, s_lo, s_hi) * (1.0 / A_HEAD_DIM)
        y = x * lax.rsqrt(ms + EPS) * g128
        if scale != 1.0:
            y = y * scale
        o_ref[c] = y.astype(o_ref.dtype)


def _in_proj_body(x_ref, nm_ref, wqa, wka, wva, wqb, wkb, wvb, wrb, wab, wqc, qna, kna, qnc, gup, gbias,
                  qa_o, ka_o, va_o, qb_o, kb_o, vb_o, rb_o, la_o, qc_o):
    xn = (_rms(x_ref[...]) * nm_ref[...]).astype(BF16)

    def mm(w):
        return jnp.dot(xn, w[...], preferred_element_type=F32)

    _headnorm64(mm(wqa), qna[...], A_HEAD_DIM ** -0.5, qa_o)
    _headnorm64(mm(wka), kna[...], 1.0, ka_o)
    hv = mm(wva)
    for c in range(A_WIDTH // LANES):
        va_o[c] = hv[:, c * LANES:(c + 1) * LANES]
    qb_o[...] = mm(wqb) * (B_DK ** -0.5)
    kb_o[...] = mm(wkb)
    vb_o[...] = mm(wvb)
    rb_o[...] = mm(wrb)
    z = jnp.dot(mm(wab), gup[...], precision=HIGHEST, preferred_element_type=F32) + gbias[...]
    la_o[...] = (jnp.minimum(z, 0.0) - jnp.log1p(jnp.exp(-jnp.abs(z)))) * (1.0 / B_TAU)
    hq = mm(wqc)
    for h in range(C_HEADS):
        sl = slice(h * C_HEAD_DIM, (h + 1) * C_HEAD_DIM)
        qc_o[:, sl] = _rms(hq[:, sl]) * qnc[...] * (C_HEAD_DIM ** -0.5)


def _in_proj(x2d, wts, tm):
    n = x2d.shape[0]
    ins = [x2d, wts['nm'], wts['wqa'], wts['wka'], wts['wva'], wts['wqb'], wts['wkb'], wts['wvb'], wts['wrb'],
           wts['wab'], wts['wqc'], wts['qna'], wts['kna'], wts['qnc'], wts['gup'], wts['gbias']]
    in_specs = [pl.BlockSpec((tm, D_MODEL), lambda i: (i, 0))] + [_const_spec(a.shape, single=False) for a in ins[1:]]
    nch = A_WIDTH // LANES
    widths = [B_KW, B_KW, B_VW, B_VW, B_KW, C_WIDTH]
    out_shape = ([jax.ShapeDtypeStruct((nch, n, LANES), F32)] * 3
                 + [jax.ShapeDtypeStruct((n, w), F32) for w in widths])
    out_specs = ([pl.BlockSpec((nch, tm, LANES), lambda i: (0, i, 0))] * 3
                 + [pl.BlockSpec((tm, w), lambda i: (i, 0)) for w in widths])
    return pl.pallas_call(
        _in_proj_body, grid=(n // tm,), in_specs=in_specs, out_specs=out_specs, out_shape=out_shape,
        compiler_params=_cparams(("parallel",)), name="in_proj")(*ins)


def _dil_prompt_body(q_ref, k_ref, v_ref, o_ref, os_ref, ls_ref, *, seq):
    g = pl.program_id(1)
    J = A_J
    row = lax.broadcasted_iota(jnp.int32, (J, J), 0)
    col = lax.broadcasted_iota(jnp.int32, (J, J), 1)
    cur_ok = col <= row
    prev_ok = col >= row
    lo = lax.broadcasted_iota(jnp.int32, (1, LANES), 1) < A_HEAD_DIM
    hi = jnp.logical_not(lo)

    for gi, (_, r) in enumerate(A_GROUPS):
        nb = seq // (r * J)

        @pl.when(g == gi)
        def _(gi=gi, r=r, nb=nb):
            def blk(i, carry):
                rho = i >> (nb.bit_length() - 1)
                n = i & (nb - 1)
                start = rho + n * (r * J)
                pstart = jnp.maximum(start - r * J, 0)
                if r == 1:
                    start, pstart = pl.multiple_of(start, J), pl.multiple_of(pstart, J)
                prev_pen = jnp.where(n > 0, 0.0, NEG).astype(F32)

                def ld(ref, c, s):
                    return ref[c, 0, pl.ds(s, J, stride=r), :]

                heads = []
                for c in range(2):
                    q = ld(q_ref, c, start)
                    kc, vc = ld(k_ref, c, start).astype(BF16), ld(v_ref, c, start).astype(BF16)
                    kp, vp = ld(k_ref, c, pstart).astype(BF16), ld(v_ref, c, pstart).astype(BF16)
                    for lanes_ok in (lo, hi):
                        qh = jnp.where(lanes_ok, q, 0.0).astype(BF16)
                        s_c = lax.dot_general(qh, kc, NT, preferred_element_type=F32)
                        s_p = lax.dot_general(qh, kp, NT, preferred_element_type=F32)
                        heads.append((s_c, s_p, vc, vp))
                outs = []
                for s_c, s_p, vc, vp in heads:
                    s_c = jnp.where(cur_ok, s_c, NEG)
                    s_p = jnp.where(prev_ok, s_p, NEG) + prev_pen
                    m = jnp.max(jnp.maximum(s_c, s_p), axis=-1, keepdims=True)
                    p_c = jnp.exp(s_c - m)
                    p_p = jnp.exp(s_p - m)
                    l = jnp.sum(p_c + p_p, axis=-1, keepdims=True)
                    pv = (jnp.dot(p_c.astype(BF16), vc, preferred_element_type=F32)
                          + jnp.dot(p_p.astype(BF16), vp, preferred_element_type=F32))
                    outs.append((pv / l, m + jnp.log(l)))
                for c in range(2):
                    (o0, l0), (o1, l1) = outs[2 * c], outs[2 * c + 1]
                    os_ref[gi, c, pl.ds(start, J, stride=r), :] = jnp.where(lo, o0, o1)
                    ls_ref[gi, c, pl.ds(start, J, stride=r), :] = jnp.where(lo, l0, l1)
                return carry

            lax.fori_loop(0, seq // J, blk, 0)

    @pl.when(g == len(A_GROUPS) - 1)
    def _():
        def comb(i, carry):
            sl = pl.ds(pl.multiple_of(i * J, J), J)
            for c in range(2):
                l0, l1, l2 = ls_ref[0, c, sl, :], ls_ref[1, c, sl, :], ls_ref[2, c, sl, :]
                mx = jnp.maximum(jnp.maximum(l0, l1), l2)
                w0, w1, w2 = jnp.exp(l0 - mx), jnp.exp(l1 - mx), jnp.exp(l2 - mx)
                num = w0 * os_ref[0, c, sl, :] + w1 * os_ref[1, c, sl, :] + w2 * os_ref[2, c, sl, :]
                o_ref[0, sl, c * LANES:(c + 1) * LANES] = (num / (w0 + w1 + w2)).astype(o_ref.dtype)
            return carry

        lax.fori_loop(0, seq // J, comb, 0)


def _dilated_prompt(qa, ka, va):
    _, bsz, seq, _ = qa.shape
    ng = len(A_GROUPS)
    spec = pl.BlockSpec((2, 1, seq, LANES), lambda b, g: (g, b, 0, 0))
    return pl.pallas_call(
        functools.partial(_dil_prompt_body, seq=seq), grid=(bsz, ng),
        in_specs=[spec, spec, spec],
        out_specs=pl.BlockSpec((1, seq, A_GW), lambda b, g: (b, 0, 0)),
        out_shape=jax.ShapeDtypeStruct((bsz, seq, A_GW), BF16),
        scratch_shapes=[pltpu.VMEM((ng, 2, seq, LANES), F32), pltpu.VMEM((ng, 2, seq, LANES), F32)],
        compiler_params=_cparams(("parallel", "arbitrary")), name="dilated_prompt")(qa, ka, va)


def _dil_sample_consts(n_new):
    ecat = np.zeros((n_new * A_GW, LANES), np.float32)
    for l in range(n_new):
        for h in range(A_HEADS):
            ecat[l * A_GW + h * A_HEAD_DIM: l * A_GW + (h + 1) * A_HEAD_DIM, A_HEADS * l + h] = 1.0
    return jnp.asarray(ecat, BF16), jnp.asarray(ecat.T, BF16)


def _dil_sample_body(q_ref, k_ref, v_ref, c1_ref, c2_ref, c3_ref, ecat_ref, etcat_ref,
                     o_ref, n1_ref, n2_ref, n3_ref, *, bb, n_new):
    L = n_new
    J = A_J
    caches = (c1_ref, c2_ref, c3_ref)
    outs = (n1_ref, n2_ref, n3_ref)
    ecat = ecat_ref[...]
    etcat = etcat_ref[...]
    hshift = A_HEADS.bit_length() - 1
    lane_q = lax.broadcasted_iota(jnp.int32, (J, LANES), 1) >> hshift
    row_c = lax.broadcasted_iota(jnp.int32, (J, LANES), 0)
    lane_qn = lax.broadcasted_iota(jnp.int32, (2 * L, LANES), 1) >> hshift
    row_n = lax.broadcasted_iota(jnp.int32, (2 * L, LANES), 0)

    NC = 2 * A_GW // LANES
    zeros_new = jnp.zeros((L, LANES), F32)
    for b in range(bb):
        s_cache, s_new, v_rows, v_news = [], [], [], []
        for gi, (w, r) in enumerate(A_GROUPS):
            cref = caches[gi]
            qg = [q_ref[2 * gi + c, b] for c in range(2)]
            kn = [k_ref[2 * gi + c, b] for c in range(2)]
            vn = [v_ref[2 * gi + c, b] for c in range(2)]
            outs[gi][b, 0:(w - L) * NC, :] = cref[b, L * NC:w * NC, :]
            for c, val in enumerate(kn + vn):
                outs[gi][b, pl.ds((w - L) * NC + c, L, stride=NC), :] = val
            kres, vres = {}, {}
            for rho in range(min(r, L)):
                kres[rho] = [cref[b, pl.ds(rho * NC + c, J, stride=r * NC), :] for c in range(2)]
                vres[rho] = [cref[b, pl.ds(rho * NC + 2 + c, J, stride=r * NC), :] for c in range(2)]
            kn16 = [jnp.concatenate([x, zeros_new], axis=0) for x in kn]
            prods, prods_n = [], []
            for l in range(L):
                for c in range(2):
                    ql = qg[c][l:l + 1, :]
                    prods.append((kres[l % r][c] * ql).astype(BF16))
                    prods_n.append((kn16[c] * ql).astype(BF16))
            sc = jnp.dot(jnp.concatenate(prods, axis=-1), ecat, preferred_element_type=F32)
            sn = jnp.dot(jnp.concatenate(prods_n, axis=-1), ecat, preferred_element_type=F32)
            sc = jnp.where(row_c >= (lane_q >> (r.bit_length() - 1)), sc, NEG)
            ok_n = (row_n <= lane_qn) & (((lane_qn - row_n) & (r - 1)) == 0) & (row_n < L)
            sn = jnp.where(ok_n, sn, NEG)
            s_cache.append(sc)
            s_new.append(sn)
            v_rows.append(vres)
            v_news.append([jnp.concatenate([x, zeros_new], axis=0) for x in vn])
        m = None
        for s in s_cache + s_new:
            ms = jnp.max(s, axis=0, keepdims=True)
            m = ms if m is None else jnp.maximum(m, ms)
        p_cache = [jnp.exp(s - m) for s in s_cache]
        p_new = [jnp.exp(s - m) for s in s_new]
        den = sum(jnp.sum(p, axis=0, keepdims=True) for p in p_cache + p_new)
        inv = 1.0 / den
        acc = [[jnp.zeros((1, LANES), F32) for _ in range(2)] for _ in range(L)]
        for gi, (w, r) in enumerate(A_GROUPS):
            pb = jnp.dot((p_cache[gi] * inv).astype(BF16), etcat, preferred_element_type=F32)
            pbn = jnp.dot((p_new[gi] * inv).astype(BF16), etcat, preferred_element_type=F32)
            for l in range(L):
                for c in range(2):
                    lsl = slice(l * A_GW + c * LANES, l * A_GW + (c + 1) * LANES)
                    acc[l][c] = (acc[l][c] + jnp.sum(pb[:, lsl] * v_rows[gi][l % r][c], axis=0, keepdims=True)
                                 + jnp.sum(pbn[:, lsl] * v_news[gi][c], axis=0, keepdims=True))
        for l in range(L):
            for c in range(2):
                o_ref[b, l:l + 1, c * LANES:(c + 1) * LANES] = acc[l][c].astype(o_ref.dtype)


def _dilated_sample(qa, ka, va, c1, c2, c3, bb):
    _, bd, n_new, _ = qa.shape
    ecat, etcat = _dil_sample_consts(n_new)
    new_spec = pl.BlockSpec((A_WIDTH // LANES, bb, n_new, LANES), lambda i: (0, i, 0, 0))
    cspecs = [pl.BlockSpec((bb,) + c.shape[1:], lambda i: (i, 0, 0)) for c in (c1, c2, c3)]
    return pl.pallas_call(
        functools.partial(_dil_sample_body, bb=bb, n_new=n_new), grid=(bd // bb,),
        in_specs=[new_spec, new_spec, new_spec] + cspecs + [_const_spec(ecat.shape), _const_spec(etcat.shape)],
        out_specs=[pl.BlockSpec((bb, n_new, A_GW), lambda i: (i, 0, 0))] + cspecs,
        out_shape=[jax.ShapeDtypeStruct((bd, n_new, A_GW), F32)]
        + [jax.ShapeDtypeStruct(c.shape, F32) for c in (c1, c2, c3)],
        compiler_params=_cparams(("parallel",)), name="dilated_sample")(qa, ka, va, c1, c2, c3, ecat, etcat)


def _gla_body(q_ref, k_ref, v_ref, la_ref, rb_ref, gn_ref, s0_ref, o_ref, sf_ref, s_ref, *, bb, tc, chunk):
    t = pl.program_id(1)
    C = chunk

    @pl.when(t == 0)
    def _():
        s_ref[...] = s0_ref[...]

    causal = lax.broadcasted_iota(jnp.int32, (C, C), 0) >= lax.broadcasted_iota(jnp.int32, (C, C), 1)
    lo = lax.broadcasted_iota(jnp.int32, (1, LANES), 1) < B_DK
    head_lanes = (lo, jnp.logical_not(lo))
    row_s = lax.broadcasted_iota(jnp.int32, (LANES, LANES), 0)
    eye = row_s == lax.broadcasted_iota(jnp.int32, (LANES, LANES), 1)
    top = row_s < B_DK
    pos = lax.broadcasted_iota(jnp.int32, (tc, 1), 0) & (C - 1)
    mid = C // 2 - 1
    gn = gn_ref[...]
    nc = tc // C

    def per_chunk_row(x, r):
        return jnp.concatenate([jnp.broadcast_to(x[c * C + r:c * C + r + 1, :], (C, x.shape[1])) for c in range(nc)],
                               axis=0)

    for b in range(bb):
        cum = la_ref[b]
        step = 1
        while step < C:
            cum = cum + jnp.where(pos >= step, pltpu.roll(cum, step, axis=0), 0.0)
            step *= 2
        cmid = per_chunk_row(cum, mid)
        clast = per_chunk_row(cum, C - 1)
        q = q_ref[b]
        k = k_ref[b]
        q_in = q * jnp.exp(cum)
        q_t = q * jnp.exp(cum - cmid)
        k_t = (k * jnp.exp(cmid - cum)).astype(BF16)
        k_l = (k * jnp.exp(clast - cum)).astype(BF16)
        e_last = jnp.exp(clast)

        o_intra = {}
        upd = {}
        for c in range(nc):
            rows = slice(c * C, (c + 1) * C)
            for p in range(B_HEADS // 2):
                pl_ = slice(p * LANES, (p + 1) * LANES)
                kl = k_l[rows, pl_]
                tn = []
                for hh in range(2):
                    h = 2 * p + hh
                    v = v_ref[b, rows, h * B_DV:(h + 1) * B_DV].astype(BF16)
                    qt = jnp.where(head_lanes[hh], q_t[rows, pl_], 0.0).astype(BF16)
                    att = lax.dot_general(qt, k_t[rows, pl_], NT, preferred_element_type=F32)
                    att = jnp.where(causal, att, 0.0).astype(BF16)
                    o_intra[c, h] = jnp.dot(att, v, preferred_element_type=F32)
                    tn.append(lax.dot_general(kl, v, TN, preferred_element_type=F32))
                upd[c, p] = jnp.where(top, tn[0], tn[1])

        for c in range(nc):
            rows = slice(c * C, (c + 1) * C)
            for p in range(B_HEADS // 2):
                pl_ = slice(p * LANES, (p + 1) * LANES)
                S = s_ref[b, p]
                S16 = S.astype(BF16)
                e_col = jnp.sum(jnp.where(eye, e_last[c * C:c * C + 1, pl_], 0.0), axis=-1, keepdims=True)
                s_ref[b, p] = e_col * S + upd[c, p]
                for hh in range(2):
                    h = 2 * p + hh
                    vs = slice(h * B_DV, (h + 1) * B_DV)
                    qi = jnp.where(head_lanes[hh], q_in[rows, pl_], 0.0).astype(BF16)
                    o = o_intra[c, h] + jnp.dot(qi, S16, preferred_element_type=F32)
                    mu = jnp.mean(o, axis=-1, keepdims=True)
                    d = o - mu
                    var = jnp.mean(d * d, axis=-1, keepdims=True)
                    rb = rb_ref[b, rows, vs]
                    o_ref[b, rows, vs] = (d * lax.rsqrt(var + EPS) * gn * (rb * jax.nn.sigmoid(rb))).astype(o_ref.dtype)

    @pl.when(t == pl.num_programs(1) - 1)
    def _():
        sf_ref[...] = s_ref[...]


def _gla(qb, kb, vb, la, rb, gn, s0, bb, tc):
    bsz, seq, _ = qb.shape
    chunk = int(np.gcd(seq, B_CHUNK))
    kspec = pl.BlockSpec((bb, tc, B_KW), lambda i, t: (i, t, 0))
    vspec = pl.BlockSpec((bb, tc, B_VW), lambda i, t: (i, t, 0))
    pair_shape = (B_HEADS // 2, 2 * B_DK, B_DV)
    sspec = pl.BlockSpec((bb,) + pair_shape, lambda i, t: (i, 0, 0, 0))
    o, s_fin = pl.pallas_call(
        functools.partial(_gla_body, bb=bb, tc=tc, chunk=chunk), grid=(bsz // bb, seq // tc),
        in_specs=[kspec, kspec, vspec, kspec, vspec, _const_spec(gn.shape), sspec],
        out_specs=[vspec, sspec],
        out_shape=[jax.ShapeDtypeStruct((bsz, seq, B_VW), BF16), jax.ShapeDtypeStruct((bsz,) + pair_shape, F32)],
        scratch_shapes=[pltpu.VMEM((bb,) + pair_shape, F32)],
        compiler_params=_cparams(("parallel", "arbitrary")), name="gla")(
            qb, kb, vb, la, rb, gn, s0.reshape((bsz,) + pair_shape))
    return o, s_fin.reshape(s0.shape)


def _mem_kv_body(m_ref, mn_ref, w_ref, knc_ref, o_ref):
    xn = (_rms(m_ref[...]) * mn_ref[...]).astype(BF16)
    kv = jnp.dot(xn, w_ref[...], preferred_element_type=F32)
    for h in range(C_HEADS):
        sl = slice(h * C_HEAD_DIM, (h + 1) * C_HEAD_DIM)
        o_ref[:, sl] = _rms(kv[:, sl]) * knc_ref[...]
    o_ref[:, C_WIDTH:] = kv[:, C_WIDTH:]


def _mem_kv(mem2d, mn, w, knc, tm):
    n = mem2d.shape[0]
    return pl.pallas_call(
        _mem_kv_body, grid=(n // tm,),
        in_specs=[pl.BlockSpec((tm, D_MODEL), lambda i: (i, 0)), _const_spec(mn.shape), _const_spec(w.shape),
                  _const_spec(knc.shape)],
        out_specs=pl.BlockSpec((tm, 2 * C_WIDTH), lambda i: (i, 0)),
        out_shape=jax.ShapeDtypeStruct((n, 2 * C_WIDTH), F32),
        compiler_params=_cparams(("parallel",)), name="mem_kv")(mem2d, mn, w, knc)


def _cross_body(q_ref, mem_ref, o_ref, *, bb, head_rows):
    for b in range(bb):
        q = q_ref[b].astype(BF16)
        for h in range(C_HEADS):
            sl = slice(h * C_HEAD_DIM, (h + 1) * C_HEAD_DIM)
            if head_rows:
                kh = mem_ref[b, pl.ds(h, N_MEM, stride=2 * C_HEADS), :].astype(BF16)
                vh = mem_ref[b, pl.ds(C_HEADS + h, N_MEM, stride=2 * C_HEADS), :].astype(BF16)
            else:
                kh = mem_ref[b, :, sl].astype(BF16)
                vh = mem_ref[b, :, C_WIDTH + h * C_HEAD_DIM:C_WIDTH + (h + 1) * C_HEAD_DIM].astype(BF16)
            s = lax.dot_general(q[:, sl], kh, NT, preferred_element_type=F32)
            p = jnp.exp(s - jnp.max(s, axis=-1, keepdims=True))
            l = jnp.sum(p, axis=-1, keepdims=True)
            o = jnp.dot(p.astype(BF16), vh, preferred_element_type=F32) / l
            o_ref[b, :, sl] = o.astype(o_ref.dtype)


def _cross(qc, mem_kv, bb, tq):
    bsz, seq, _ = qc.shape
    head_rows = mem_kv.shape[-1] == C_HEAD_DIM
    return pl.pallas_call(
        functools.partial(_cross_body, bb=bb, head_rows=head_rows), grid=(bsz // bb, seq // tq),
        in_specs=[pl.BlockSpec((bb, tq, C_WIDTH), lambda i, t: (i, t, 0)),
                  pl.BlockSpec((bb,) + mem_kv.shape[1:], lambda i, t: (i, 0, 0))],
        out_specs=pl.BlockSpec((bb, tq, C_WIDTH), lambda i, t: (i, t, 0)),
        out_shape=jax.ShapeDtypeStruct((bsz, seq, C_WIDTH), F32),
        compiler_params=_cparams(("parallel", "arbitrary")), name="cross")(qc, mem_kv)


def _merge_body(x_ref, nm_ref, wgl, oa_ref, ob_ref, oc_ref, wba, wbb, wbc, wout, nf_ref, wr, br,
                h_o, xt_o, route_o):
    x = x_ref[...]
    xn = (_rms(x) * nm_ref[...]).astype(BF16)
    merged = None
    for i, (o_ref, w_ref) in enumerate(((oa_ref, wba), (ob_ref, wbb), (oc_ref, wbc))):
        gate = jax.nn.sigmoid(jnp.dot(xn, wgl[:, i * D_MODEL:(i + 1) * D_MODEL], preferred_element_type=F32))
        term = gate * jnp.dot(o_ref[...].astype(BF16), w_ref[...], preferred_element_type=F32)
        merged = term if merged is None else merged + term
    h = x + jnp.dot(merged.astype(BF16), wout[...], preferred_element_type=F32)
    h_o[...] = h
    xt = _rms(h) * nf_ref[...]
    xt_o[...] = xt
    xt_hi = xt.astype(BF16)
    xt_lo = (xt - xt_hi.astype(F32)).astype(BF16)
    logits = (jnp.dot(xt_hi, wr[0], preferred_element_type=F32) + jnp.dot(xt_hi, wr[1], preferred_element_type=F32)
              + jnp.dot(xt_lo, wr[0], preferred_element_type=F32) + br[...])
    lane = lax.broadcasted_iota(jnp.int32, logits.shape, 1)
    lane_f = lane.astype(F32)
    big = float(LANES)
    is_g = (lane >= N_EXPERTS) & (lane < N_EXPERTS + N_GROUPS_E)
    glog = jnp.where(is_g, logits, NEG)
    gmax = jnp.max(glog, axis=-1, keepdims=True)
    grp = jnp.min(jnp.where(glog == gmax, lane_f, big), axis=-1, keepdims=True) - float(N_EXPERTS)
    g_w = 1.0 / jnp.sum(jnp.where(is_g, jnp.exp(glog - gmax), 0.0), axis=-1, keepdims=True)
    in_grp = (lane < N_EXPERTS) & (jnp.floor(lane_f * (1.0 / E_PER_GROUP)) == grp)
    elog = jnp.where(in_grp, logits, NEG)
    v1 = jnp.max(elog, axis=-1, keepdims=True)
    i1 = jnp.min(jnp.where(elog == v1, lane_f, big), axis=-1, keepdims=True)
    elog2 = jnp.where(lane_f == i1, NEG, elog)
    v2 = jnp.max(elog2, axis=-1, keepdims=True)
    i2 = jnp.min(jnp.where(elog2 == v2, lane_f, big), axis=-1, keepdims=True)
    e2 = jnp.exp(v2 - v1)
    w1 = g_w / (1.0 + e2)
    w2 = g_w * e2 / (1.0 + e2)
    route_o[...] = jnp.where(lane == 0, i1, jnp.where(lane == 1, i2, jnp.where(lane == 2, w1,
                             jnp.where(lane == 3, w2, 0.0))))


def _merge(x2d, oa, ob, oc, wts, tm):
    n = x2d.shape[0]
    consts = [wts['nm'], wts['wgl']]
    consts2 = [wts['wba'], wts['wbb'], wts['wbc'], wts['wout'], wts['nf'], wts['wr'], wts['br']]

    def tile(w):
        return pl.BlockSpec((tm, w), lambda i: (i, 0))

    in_specs = ([tile(D_MODEL)] + [_const_spec(a.shape) for a in consts]
                + [tile(A_GW), tile(B_VW), tile(C_WIDTH)] + [_const_spec(a.shape) for a in consts2])
    return pl.pallas_call(
        _merge_body, grid=(n // tm,), in_specs=in_specs,
        out_specs=[tile(D_MODEL), tile(D_MODEL), tile(LANES)],
        out_shape=[jax.ShapeDtypeStruct((n, D_MODEL), F32), jax.ShapeDtypeStruct((n, D_MODEL), F32),
                   jax.ShapeDtypeStruct((n, LANES), F32)],
        compiler_params=_cparams(("parallel",)), name="merge")(x2d, *consts, oa, ob, oc, *consts2)


ROW_DMA_UNROLL = 8


def _dispatch_body(dest_ref, xt_ref, xs_in, xs_out, stage, sem, *, tm):
    del xs_in
    j = pl.program_id(0)
    nblk = pl.num_programs(0)
    slot = j % 2

    def wait_slot(s):
        for _ in range(2):
            pltpu.make_async_copy(stage.at[s], xs_out.at[pl.ds(0, tm)], sem.at[s]).wait()

    @pl.when(j >= 2)
    def _():
        wait_slot(slot)

    stage[slot] = xt_ref[...]
    for k in range(2):
        def body(i, carry, k=k):
            pltpu.make_async_copy(stage.at[slot, pl.ds(i, 1)], xs_out.at[pl.ds(dest_ref[0, k * tm + i], 1)],
                                  sem.at[slot]).start()
            return carry
        lax.fori_loop(0, tm, body, 0, unroll=ROW_DMA_UNROLL)

    @pl.when(j == nblk - 1)
    def _():
        wait_slot(slot)

        @pl.when(j >= 1)
        def _():
            wait_slot(1 - slot)


def _dispatch(xt, dest3, xs, tm):
    n = xt.shape[0]
    n_slots = xs.shape[0]
    return pl.pallas_call(
        functools.partial(_dispatch_body, tm=tm), grid=(n // tm,),
        in_specs=[pl.BlockSpec((None, 1, 2 * tm), lambda j: (j, 0, 0), memory_space=pltpu.SMEM),
                  pl.BlockSpec((tm, D_MODEL), lambda j: (j, 0)),
                  pl.BlockSpec(memory_space=pl.ANY)],
        out_specs=pl.BlockSpec(memory_space=pl.ANY),
        out_shape=jax.ShapeDtypeStruct((n_slots, D_MODEL), F32),
        scratch_shapes=[pltpu.VMEM((2, tm, D_MODEL), F32), pltpu.SemaphoreType.DMA((2,))],
        input_output_aliases={2: 0},
        compiler_params=_cparams(("arbitrary",)), name="dispatch")(dest3, xt, xs)


def _expert_body(be_ref, xs_ref, wgu_ref, wd_ref, y_ref):
    del be_ref
    gu = jnp.dot(xs_ref[...].astype(BF16), wgu_ref[0], preferred_element_type=F32)
    gate = gu[:, :D_EXPERT]
    hid = gate * jax.nn.sigmoid(gate) * gu[:, D_EXPERT:]
    y_ref[...] = jnp.dot(hid.astype(BF16), wd_ref[0], preferred_element_type=F32)


def _experts(xs, block_e, wgu, wd, bm):
    nblk = block_e.shape[0]
    grid_spec = pltpu.PrefetchScalarGridSpec(
        num_scalar_prefetch=1, grid=(nblk,),
        in_specs=[
            pl.BlockSpec((bm, D_MODEL), lambda j, be: (j, 0)),
            pl.BlockSpec((1, D_MODEL, 2 * D_EXPERT), lambda j, be: (be[j], 0, 0)),
            pl.BlockSpec((1, D_EXPERT, D_MODEL), lambda j, be: (be[j], 0, 0)),
        ],
        out_specs=pl.BlockSpec((bm, D_MODEL), lambda j, be: (j, 0)))
    return pl.pallas_call(
        _expert_body, grid_spec=grid_spec,
        out_shape=jax.ShapeDtypeStruct((nblk * bm, D_MODEL), F32),
        compiler_params=_cparams(("parallel",)), name="experts")(block_e, xs, wgu, wd)


def _combine_body(d_cur, d_next, h_ref, route_ref, ys_hbm, y_ref, ybuf, sem, *, tm):
    j = pl.program_id(0)
    nblk = pl.num_programs(0)
    slot = j % 2

    def gather(idx_ref, s):
        def body(i, carry):
            pltpu.make_async_copy(ys_hbm.at[pl.ds(idx_ref[0, i], 1)], ybuf.at[s, pl.ds(i, 1)], sem.at[s]).start()
            return carry
        lax.fori_loop(0, 2 * tm, body, 0, unroll=ROW_DMA_UNROLL)

    @pl.when(j == 0)
    def _():
        gather(d_cur, 0)

    @pl.when(j + 1 < nblk)
    def _():
        gather(d_next, 1 - slot)

    pltpu.make_async_copy(ys_hbm.at[pl.ds(0, 2 * tm)], ybuf.at[slot], sem.at[slot]).wait()
    route = route_ref[...]
    w1 = route[:, 2:3]
    w2 = route[:, 3:4]
    y_ref[...] = h_ref[...] + (ybuf[slot, 0:tm, :] * w1 + ybuf[slot, tm:2 * tm, :] * w2)


def _tile_slots(dest, tm):
    nblk = dest.shape[0] // tm
    return dest.reshape(nblk, tm, 2).transpose(0, 2, 1).reshape(nblk, 1, 2 * tm)


def _combine(h, route, ys, dest, tm):
    n = h.shape[0]
    nblk = n // tm
    d3 = _tile_slots(dest, tm)
    return pl.pallas_call(
        functools.partial(_combine_body, tm=tm), grid=(nblk,),
        in_specs=[
            pl.BlockSpec((None, 1, 2 * tm), lambda j: (j, 0, 0), memory_space=pltpu.SMEM),
            pl.BlockSpec((None, 1, 2 * tm), lambda j: (jnp.minimum(j + 1, nblk - 1), 0, 0), memory_space=pltpu.SMEM),
            pl.BlockSpec((tm, D_MODEL), lambda j: (j, 0)),
            pl.BlockSpec((tm, LANES), lambda j: (j, 0)),
            pl.BlockSpec(memory_space=pl.ANY),
        ],
        out_specs=pl.BlockSpec((tm, D_MODEL), lambda j: (j, 0)),
        out_shape=jax.ShapeDtypeStruct((n, D_MODEL), F32),
        scratch_shapes=[pltpu.VMEM((2, 2 * tm, D_MODEL), F32), pltpu.SemaphoreType.DMA((2,))],
        compiler_params=_cparams(("arbitrary",)), name="combine")(d3, d3, h, route, ys, )


def _dispatch_plan(route, bm):
    n = route.shape[0]
    flat_e = route[:, 0:2].astype(jnp.int32).reshape(-1)
    n_asg = flat_e.shape[0]
    onehot = (flat_e[:, None] == jnp.arange(N_EXPERTS, dtype=jnp.int32)[None, :]).astype(jnp.int32)
    csum = jnp.cumsum(onehot, axis=0)
    rank = jnp.sum(csum * onehot, axis=1) - 1
    counts = csum[-1]
    padded = (counts + bm - 1) // bm * bm
    pad_end = jnp.cumsum(padded)
    dest = jnp.sum((pad_end - padded)[None, :] * onehot, axis=1) + rank
    nblk = -(-n_asg // bm) + N_EXPERTS
    first_slot = jnp.arange(nblk, dtype=jnp.int32) * bm
    block_e = jnp.minimum(jnp.sum((pad_end[None, :] <= first_slot[:, None]).astype(jnp.int32), axis=1),
                          N_EXPERTS - 1)
    return dest.reshape(n, 2), block_e


def _hmoe(groups, wts, bm, tm):
    route_all = jnp.concatenate([route for _, _, route in groups], axis=0)
    dest_all, block_e = _dispatch_plan(route_all, bm)
    xs = jnp.zeros((block_e.shape[0] * bm, D_MODEL), F32)
    dests, off = [], 0
    for _, xt, _ in groups:
        n = xt.shape[0]
        dests.append(dest_all[off:off + n])
        off += n
        xs = _dispatch(xt, _tile_slots(dests[-1], _tile(n, tm)), xs, _tile(n, tm))
    ys = _experts(xs, block_e, wts['wgu'], wts['wd'], bm)
    return [_combine(h, route, ys, dest, _tile(h.shape[0], tm)) for (h, _, route), dest in zip(groups, dests)]


def _prep_weights(norm_mix, w_in, qn_a, kn_a, qn_c, kn_c, gla_gate_up, gla_gate_bias, gla_norm, mem_norm,
                  w_mem_kv, w_branch_a, w_branch_b, w_branch_c, w_out, norm_ffn, w_router_group, b_router_group,
                  w_router_expert, b_router_expert, w_exp_gate, w_exp_up, w_exp_down):
    splits = (A_WIDTH, A_WIDTH, A_WIDTH, B_KW, B_KW, B_VW, B_VW, B_RANK, C_WIDTH, 3 * D_MODEL)
    offs = np.concatenate([[0], np.cumsum(splits)])
    names = ('wqa', 'wka', 'wva', 'wqb', 'wkb', 'wvb', 'wrb', 'wab', 'wqc', 'wgl')
    wts = {nm: w_in[:, int(offs[i]):int(offs[i + 1])].astype(BF16) for i, nm in enumerate(names)}
    wts['wab'] = jnp.pad(wts['wab'], ((0, 0), (0, LANES - B_RANK)))
    wts['gup'] = jnp.pad(gla_gate_up.astype(F32), ((0, LANES - B_RANK), (0, 0)))
    wts['gbias'] = gla_gate_bias.reshape(1, B_KW)
    wts['nm'] = norm_mix.reshape(1, D_MODEL)
    wts['qna'] = jnp.tile(qn_a, 2).reshape(1, LANES)
    wts['kna'] = jnp.tile(kn_a, 2).reshape(1, LANES)
    wts['qnc'] = qn_c.reshape(1, C_HEAD_DIM)
    wts['knc'] = kn_c.reshape(1, C_HEAD_DIM)
    wts['gn'] = gla_norm.reshape(1, B_DV)
    wts['mn'] = mem_norm.reshape(1, D_MODEL)
    wts['wmem'] = w_mem_kv.astype(BF16)
    wts['wba'] = w_branch_a.astype(BF16)
    wts['wbb'] = w_branch_b.astype(BF16)
    wts['wbc'] = w_branch_c.astype(BF16)
    wts['wout'] = w_out.astype(BF16)
    wts['nf'] = norm_ffn.reshape(1, D_MODEL)
    wr = jnp.concatenate([w_router_expert, w_router_group], axis=1)
    br = jnp.concatenate([b_router_expert, b_router_group])
    npad = LANES - N_EXPERTS - N_GROUPS_E
    wr = jnp.pad(wr, ((0, 0), (0, npad)))
    wr_hi = wr.astype(BF16)
    wts['wr'] = jnp.stack([wr_hi, (wr - wr_hi.astype(F32)).astype(BF16)])
    wts['br'] = jnp.pad(br, (0, npad)).reshape(1, LANES)
    wts['wgu'] = jnp.concatenate([w_exp_gate, w_exp_up], axis=2).astype(BF16)
    wts['wd'] = w_exp_down.astype(BF16)
    return wts


def _tile(n, want):
    t = min(n, want)
    assert n % t == 0
    return t


def _layer(x, wts, attend, gla_state, mem_kv, cfg):
    bsz, seq, _ = x.shape
    n = bsz * seq
    x2d = x.reshape(n, D_MODEL)
    qa, ka, va, qb, kb, vb, rb, la, qc = _in_proj(x2d, wts, _tile(n, cfg['tm_in']))

    def r3(a):
        return a.reshape(bsz, seq, a.shape[-1])

    def r4(a):
        return a.reshape(a.shape[0], bsz, seq, LANES)

    oa, extras = attend(r4(qa), r4(ka), r4(va))
    ob, s_fin = _gla(r3(qb), r3(kb), r3(vb), r3(la), r3(rb), wts['gn'], gla_state, cfg['gla_bb'],
                     _tile(seq, cfg['gla_tc']))
    oc = _cross(r3(qc), mem_kv, cfg['cross_bb'], _tile(seq, cfg['cross_tq']))
    pre_ffn = _merge(x2d, oa.reshape(n, A_GW), ob.reshape(n, B_VW), oc.reshape(n, C_WIDTH), wts,
                     _tile(n, cfg['tm_merge']))
    return pre_ffn, extras, s_fin


def kernel(x_prompt, x_sample, mem_prompt, cache_win1, cache_win2, cache_win3, state_gla, cache_mem, norm_mix, w_in, qn_a, kn_a, qn_c, kn_c, gla_gate_up, gla_gate_bias, gla_norm, mem_norm, w_mem_kv, w_branch_a, w_branch_b, w_branch_c, w_out, norm_ffn, w_router_group, b_router_group, w_router_expert, b_router_expert, w_exp_gate, w_exp_up, w_exp_down):
    wts = _prep_weights(norm_mix, w_in, qn_a, kn_a, qn_c, kn_c, gla_gate_up, gla_gate_bias, gla_norm, mem_norm,
                        w_mem_kv, w_branch_a, w_branch_b, w_branch_c, w_out, norm_ffn, w_router_group,
                        b_router_group, w_router_expert, b_router_expert, w_exp_gate, w_exp_up, w_exp_down)
    bsz, seq, _ = x_prompt.shape
    bd, n_new, _ = x_sample.shape

    mem_p = _mem_kv(mem_prompt.reshape(bsz * N_MEM, D_MODEL), wts['mn'], wts['wmem'], wts['knc'], 256)
    mem_p = mem_p.reshape(bsz, N_MEM, 2 * C_WIDTH)

    def attend_prompt(qa, ka, va):
        oa = _dilated_prompt(qa, ka, va)
        bufs = []
        for gi, (w, _) in enumerate(A_GROUPS):
            wb = min(w, seq)
            kv = jnp.concatenate([a[2 * gi + c, :, seq - wb:] for a in (ka, va) for c in range(2)], axis=-1)
            bufs.append(kv.reshape(bsz, wb, 2, A_HEADS, A_HEAD_DIM))
        return oa, bufs

    cfg_p = dict(tm_in=256, tm_merge=512, gla_bb=2, gla_tc=256, cross_bb=1, cross_tq=512)
    s0 = jnp.zeros((bsz, B_HEADS, B_DK, B_DV), F32)
    ffn_in_p, wins_p, gla_p = _layer(x_prompt, wts, attend_prompt, s0, mem_p, cfg_p)

    def attend_sample(qa, ka, va):
        caches = [c.reshape(bd, c.shape[1] * 2 * A_GW // LANES, LANES) for c in (cache_win1, cache_win2, cache_win3)]
        oa, n1, n2, n3 = _dilated_sample(qa, ka, va, *caches, bb=2)
        return oa, [n.reshape(c.shape) for n, c in zip((n1, n2, n3), (cache_win1, cache_win2, cache_win3))]

    cfg_s = dict(tm_in=256, tm_merge=256, gla_bb=8, gla_tc=n_new, cross_bb=8, cross_tq=n_new)
    ffn_in_s, wins_s, gla_s = _layer(x_sample, wts, attend_sample, state_gla,
                                     cache_mem.reshape(bd, N_MEM * 2 * C_HEADS, C_HEAD_DIM), cfg_s)

    y_p, y_s = _hmoe([ffn_in_p, ffn_in_s], wts, bm=256, tm=256)
    y_p = y_p.reshape(x_prompt.shape)
    y_s = y_s.reshape(x_sample.shape)

    return (y_p, y_s, wins_p[0], wins_p[1], wins_p[2], gla_p, mem_p.reshape(bsz, N_MEM, 2, C_HEADS, C_HEAD_DIM),
            wins_s[0], wins_s[1], wins_s[2], gla_s)
```

```python
import functools

import jax
import jax.numpy as jnp
import numpy as np
from jax import lax
from jax.experimental import pallas as pl
from jax.experimental.pallas import tpu as pltpu

F32 = jnp.float32
BF16 = jnp.bfloat16
HIGHEST = lax.Precision.HIGHEST

D_MODEL = 1024
A_GROUPS = ((128, 1), (512, 4), (2048, 16))
A_J = 128
A_HEADS = 4
A_HEAD_DIM = 64
A_GW = A_HEADS * A_HEAD_DIM
A_WIDTH = 3 * A_GW
B_HEADS = 4
B_DK = 64
B_DV = 128
B_KW = B_HEADS * B_DK
B_VW = B_HEADS * B_DV
B_RANK = 16
B_TAU = 16.0
B_CHUNK = 64
C_HEADS = 4
C_HEAD_DIM = 128
C_WIDTH = C_HEADS * C_HEAD_DIM
N_MEM = 256
N_GROUPS_E = 4
E_PER_GROUP = 8
N_EXPERTS = 32
D_EXPERT = 256
EPS = 1e-6
NEG = -1e30
LANES = 128
VMEM_LIMIT = 56 * 1024 * 1024

NT = (((1,), (1,)), ((), ()))
TN = (((0,), (0,)), ((), ()))


def _cparams(sem):
    return pltpu.CompilerParams(dimension_semantics=sem, vmem_limit_bytes=VMEM_LIMIT)


def _const_spec(shape, single=True):
    nd = len(shape)
    if single:
        return pl.BlockSpec(shape, lambda *_: (0,) * nd, pipeline_mode=pl.Buffered(1))
    return pl.BlockSpec(shape, lambda *_: (0,) * nd)


def _rms(x):
    return x * lax.rsqrt(jnp.mean(x * x, axis=-1, keepdims=True) + EPS)


def _headnorm64(h, g128, scale, o_ref):
    lo = lax.broadcasted_iota(jnp.int32, (1, LANES), 1) < A_HEAD_DIM
    for c in range(h.shape[1] // LANES):
        x = h[:, c * LANES:(c + 1) * LANES]
        x2 = x * x
        s_lo = jnp.sum(jnp.where(lo, x2, 0.0), axis=-1, keepdims=True)
        s_hi = jnp.sum(jnp.where(lo, 0.0, x2), axis=-1, keepdims=True)
        ms = jnp.where(lo, s_lo, s_hi) * (1.0 / A_HEAD_DIM)
        y = x * lax.rsqrt(ms + EPS) * g128
        if scale != 1.0:
            y = y * scale
        o_ref[c] = y.astype(o_ref.dtype)


def _in_proj_body(x_ref, nm_ref, wqa, wka, wva, wqb, wkb, wvb, wrb, wab, wqc, qna, kna, qnc, gup, gbias,
                  qa_o, ka_o, va_o, qb_o, kb_o, vb_o, rb_o, la_o, qc_o):
    xn = (_rms(x_ref[...]) * nm_ref[...]).astype(BF16)

    def mm(w):
        return jnp.dot(xn, w[...], preferred_element_type=F32)

    _headnorm64(mm(wqa), qna[...], A_HEAD_DIM ** -0.5, qa_o)
    _headnorm64(mm(wka), kna[...], 1.0, ka_o)
    hv = mm(wva)
    for c in range(A_WIDTH // LANES):
        va_o[c] = hv[:, c * LANES:(c + 1) * LANES]
    qb_o[...] = mm(wqb) * (B_DK ** -0.5)
    kb_o[...] = mm(wkb)
    vb_o[...] = mm(wvb)
    rb_o[...] = mm(wrb)
    z = jnp.dot(mm(wab), gup[...], precision=HIGHEST, preferred_element_type=F32) + gbias[...]
    la_o[...] = (jnp.minimum(z, 0.0) - jnp.log1p(jnp.exp(-jnp.abs(z)))) * (1.0 / B_TAU)
    hq = mm(wqc)
    for h in range(C_HEADS):
        sl = slice(h * C_HEAD_DIM, (h + 1) * C_HEAD_DIM)
        qc_o[:, sl] = _rms(hq[:, sl]) * qnc[...] * (C_HEAD_DIM ** -0.5)


def _in_proj(x2d, wts, tm):
    n = x2d.shape[0]
    ins = [x2d, wts['nm'], wts['wqa'], wts['wka'], wts['wva'], wts['wqb'], wts['wkb'], wts['wvb'], wts['wrb'],
           wts['wab'], wts['wqc'], wts['qna'], wts['kna'], wts['qnc'], wts['gup'], wts['gbias']]
    in_specs = [pl.BlockSpec((tm, D_MODEL), lambda i: (i, 0))] + [_const_spec(a.shape, single=False) for a in ins[1:]]
    nch = A_WIDTH // LANES
    widths = [B_KW, B_KW, B_VW, B_VW, B_KW, C_WIDTH]
    out_shape = ([jax.ShapeDtypeStruct((nch, n, LANES), F32)] * 3
                 + [jax.ShapeDtypeStruct((n, w), F32) for w in widths])
    out_specs = ([pl.BlockSpec((nch, tm, LANES), lambda i: (0, i, 0))] * 3
                 + [pl.BlockSpec((tm, w), lambda i: (i, 0)) for w in widths])
    return pl.pallas_call(
        _in_proj_body, grid=(n // tm,), in_specs=in_specs, out_specs=out_specs, out_shape=out_shape,
        compiler_params=_cparams(("parallel",)), name="in_proj")(*ins)


def _dil_prompt_body(q_ref, k_ref, v_ref, o_ref, os_ref, ls_ref, *, seq):
    g = pl.program_id(1)
    J = A_J
    row = lax.broadcasted_iota(jnp.int32, (J, J), 0)
    col = lax.broadcasted_iota(jnp.int32, (J, J), 1)
    cur_ok = col <= row
    prev_ok = col >= row
    lo = lax.broadcasted_iota(jnp.int32, (1, LANES), 1) < A_HEAD_DIM
    hi = jnp.logical_not(lo)

    for gi, (_, r) in enumerate(A_GROUPS):
        nb = seq // (r * J)

        @pl.when(g == gi)
        def _(gi=gi, r=r, nb=nb):
            def blk(i, carry):
                rho = i >> (nb.bit_length() - 1)
                n = i & (nb - 1)
                start = rho + n * (r * J)
                pstart = jnp.maximum(start - r * J, 0)
                if r == 1:
                    start, pstart = pl.multiple_of(start, J), pl.multiple_of(pstart, J)
                prev_pen = jnp.where(n > 0, 0.0, NEG).astype(F32)

                def ld(ref, c, s):
                    return ref[c, 0, pl.ds(s, J, stride=r), :]

                heads = []
                for c in range(2):
                    q = ld(q_ref, c, start)
                    kc, vc = ld(k_ref, c, start).astype(BF16), ld(v_ref, c, start).astype(BF16)
                    kp, vp = ld(k_ref, c, pstart).astype(BF16), ld(v_ref, c, pstart).astype(BF16)
                    for lanes_ok in (lo, hi):
                        qh = jnp.where(lanes_ok, q, 0.0).astype(BF16)
                        s_c = lax.dot_general(qh, kc, NT, preferred_element_type=F32)
                        s_p = lax.dot_general(qh, kp, NT, preferred_element_type=F32)
                        heads.append((s_c, s_p, vc, vp))
                outs = []
                for s_c, s_p, vc, vp in heads:
                    s_c = jnp.where(cur_ok, s_c, NEG)
                    s_p = jnp.where(prev_ok, s_p, NEG) + prev_pen
                    m = jnp.max(jnp.maximum(s_c, s_p), axis=-1, keepdims=True)
                    p_c = jnp.exp(s_c - m)
                    p_p = jnp.exp(s_p - m)
                    l = jnp.sum(p_c + p_p, axis=-1, keepdims=True)
                    pv = (jnp.dot(p_c.astype(BF16), vc, preferred_element_type=F32)
                          + jnp.dot(p_p.astype(BF16), vp, preferred_element_type=F32))
                    outs.append((pv / l, m + jnp.log(l)))
                for c in range(2):
                    (o0, l0), (o1, l1) = outs[2 * c], outs[2 * c + 1]
                    os_ref[gi, c, pl.ds(start, J, stride=r), :] = jnp.where(lo, o0, o1)
                    ls_ref[gi, c, pl.ds(start, J, stride=r), :] = jnp.where(lo, l0, l1)
                return carry

            lax.fori_loop(0, seq // J, blk, 0)

    @pl.when(g == len(A_GROUPS) - 1)
    def _():
        def comb(i, carry):
            sl = pl.ds(pl.multiple_of(i * J, J), J)
            for c in range(2):
                l0, l1, l2 = ls_ref[0, c, sl, :], ls_ref[1, c, sl, :], ls_ref[2, c, sl, :]
                mx = jnp.maximum(jnp.maximum(l0, l1), l2)
                w0, w1, w2 = jnp.exp(l0 - mx), jnp.exp(l1 - mx), jnp.exp(l2 - mx)
                num = w0 * os_ref[0, c, sl, :] + w1 * os_ref[1, c, sl, :] + w2 * os_ref[2, c, sl, :]
                o_ref[0, sl, c * LANES:(c + 1) * LANES] = (num / (w0 + w1 + w2)).astype(o_ref.dtype)
            return carry

        lax.fori_loop(0, seq // J, comb, 0)


def _dilated_prompt(qa, ka, va):
    _, bsz, seq, _ = qa.shape
    ng = len(A_GROUPS)
    spec = pl.BlockSpec((2, 1, seq, LANES), lambda b, g: (g, b, 0, 0))
    return pl.pallas_call(
        functools.partial(_dil_prompt_body, seq=seq), grid=(bsz, ng),
        in_specs=[spec, spec, spec],
        out_specs=pl.BlockSpec((1, seq, A_GW), lambda b, g: (b, 0, 0)),
        out_shape=jax.ShapeDtypeStruct((bsz, seq, A_GW), BF16),
        scratch_shapes=[pltpu.VMEM((ng, 2, seq, LANES), F32), pltpu.VMEM((ng, 2, seq, LANES), F32)],
        compiler_params=_cparams(("parallel", "arbitrary")), name="dilated_prompt")(qa, ka, va)


def _dil_sample_body(q_ref, k_ref, v_ref, c1_ref, c2_ref, c3_ref, o_ref, n1_ref, n2_ref, n3_ref, *, bb, n_new):
    L = n_new
    caches = (c1_ref, c2_ref, c3_ref)
    outs = (n1_ref, n2_ref, n3_ref)
    nq = A_HEADS * L
    head_of_lane = lax.broadcasted_iota(jnp.int32, (1, A_GW), 1) >> (A_HEAD_DIM.bit_length() - 1)
    q_of_row = lax.broadcasted_iota(jnp.int32, (nq, 1), 0) & (L - 1)
    t_new = lax.broadcasted_iota(jnp.int32, (nq, L), 1)
    lane = lax.broadcasted_iota(jnp.int32, (1, LANES), 1)
    pad_rows = jnp.zeros((LANES - L, 2 * A_GW), F32)

    def per_head_rows(x):
        return jnp.concatenate([jnp.where(head_of_lane == h, x, 0.0) for h in range(A_HEADS)], axis=0)

    for b in range(bb):
        scores, scores_n, vts, vns = [], [], [], []
        for gi, (w, r) in enumerate(A_GROUPS):
            cref = caches[gi]
            qg, kn, vn = [jnp.concatenate([ref[2 * gi, b], ref[2 * gi + 1, b]], axis=-1)
                          for ref in (q_ref, k_ref, v_ref)]
            qbd = per_head_rows(qg).astype(BF16)
            kt = cref[b, 0:A_GW, :].astype(BF16)
            s = jnp.dot(qbd, kt, preferred_element_type=F32)
            s_n = lax.dot_general(qbd, kn.astype(BF16), NT, preferred_element_type=F32)
            pos = lax.broadcasted_iota(jnp.int32, (nq, w), 1)
            s = jnp.where((pos >= q_of_row) & (((q_of_row - pos) & (r - 1)) == 0), s, NEG)
            s_n = jnp.where((t_new <= q_of_row) & (((q_of_row - t_new) & (r - 1)) == 0), s_n, NEG)
            scores.append(s)
            scores_n.append(s_n)
            vts.append(cref[b, A_GW:2 * A_GW, :].astype(BF16))
            vns.append(vn.astype(BF16))

            rolled = pltpu.roll(cref[b], w - L, axis=1)
            outs[gi][b] = rolled
            new_t = jnp.concatenate([jnp.concatenate([kn, vn], axis=-1), pad_rows], axis=0).T
            outs[gi][b, :, w - LANES:w] = jnp.where(lane < LANES - L, rolled[:, w - LANES:w],
                                                    pltpu.roll(new_t, LANES - L, axis=1))

        m = None
        for s in scores + scores_n:
            ms = jnp.max(s, axis=-1, keepdims=True)
            m = ms if m is None else jnp.maximum(m, ms)
        den = 0.0
        acc = jnp.zeros((nq, A_GW), F32)
        for s, s_n, vt, vn in zip(scores, scores_n, vts, vns):
            p = jnp.exp(s - m)
            p_n = jnp.exp(s_n - m)
            den = den + jnp.sum(p, axis=-1, keepdims=True) + jnp.sum(p_n, axis=-1, keepdims=True)
            acc = acc + lax.dot_general(p.astype(BF16), vt, NT, preferred_element_type=F32)
            acc = acc + jnp.dot(p_n.astype(BF16), vn, preferred_element_type=F32)
        acc = acc / den
        o = jnp.zeros((L, A_GW), F32)
        for h in range(A_HEADS):
            o = o + jnp.where(head_of_lane == h, acc[h * L:(h + 1) * L, :], 0.0)
        o_ref[b] = o.astype(o_ref.dtype)


def _dilated_sample(qa, ka, va, caches, bb):
    _, bd, n_new, _ = qa.shape
    cts = [c.transpose(0, 2, 3, 4, 1).reshape(bd, 2 * A_GW, c.shape[1]) for c in caches]
    new_spec = pl.BlockSpec((A_WIDTH // LANES, bb, n_new, LANES), lambda i: (0, i, 0, 0))
    cspecs = [pl.BlockSpec((bb,) + c.shape[1:], lambda i: (i, 0, 0)) for c in cts]
    oa, *nts = pl.pallas_call(
        functools.partial(_dil_sample_body, bb=bb, n_new=n_new), grid=(bd // bb,),
        in_specs=[new_spec, new_spec, new_spec] + cspecs,
        out_specs=[pl.BlockSpec((bb, n_new, A_GW), lambda i: (i, 0, 0))] + cspecs,
        out_shape=[jax.ShapeDtypeStruct((bd, n_new, A_GW), F32)] + [jax.ShapeDtypeStruct(c.shape, F32) for c in cts],
        compiler_params=_cparams(("parallel",)), name="dilated_sample")(qa, ka, va, *cts)
    news = [n.reshape(bd, 2, A_HEADS, A_HEAD_DIM, n.shape[-1]).transpose(0, 4, 1, 2, 3) for n in nts]
    return oa, news


def _gla_body(q_ref, k_ref, v_ref, la_ref, rb_ref, gn_ref, s0_ref, o_ref, sf_ref, s_ref, *, bb, tc, chunk):
    t = pl.program_id(1)
    C = chunk

    @pl.when(t == 0)
    def _():
        s_ref[...] = s0_ref[...]

    causal = lax.broadcasted_iota(jnp.int32, (C, C), 0) >= lax.broadcasted_iota(jnp.int32, (C, C), 1)
    lo = lax.broadcasted_iota(jnp.int32, (1, LANES), 1) < B_DK
    head_lanes = (lo, jnp.logical_not(lo))
    row_s = lax.broadcasted_iota(jnp.int32, (LANES, LANES), 0)
    eye = row_s == lax.broadcasted_iota(jnp.int32, (LANES, LANES), 1)
    top = row_s < B_DK
    pos = lax.broadcasted_iota(jnp.int32, (tc, 1), 0) & (C - 1)
    mid = C // 2 - 1
    gn = gn_ref[...]
    nc = tc // C

    def per_chunk_row(x, r):
        return jnp.concatenate([jnp.broadcast_to(x[c * C + r:c * C + r + 1, :], (C, x.shape[1])) for c in range(nc)],
                               axis=0)

    for b in range(bb):
        cum = la_ref[b]
        step = 1
        while step < C:
            cum = cum + jnp.where(pos >= step, pltpu.roll(cum, step, axis=0), 0.0)
            step *= 2
        cmid = per_chunk_row(cum, mid)
        clast = per_chunk_row(cum, C - 1)
        q = q_ref[b]
        k = k_ref[b]
        q_in = q * jnp.exp(cum)
        q_t = q * jnp.exp(cum - cmid)
        k_t = (k * jnp.exp(cmid - cum)).astype(BF16)
        k_l = (k * jnp.exp(clast - cum)).astype(BF16)
        e_last = jnp.exp(clast)

        o_intra = {}
        upd = {}
        for c in range(nc):
            rows = slice(c * C, (c + 1) * C)
            for p in range(B_HEADS // 2):
                pl_ = slice(p * LANES, (p + 1) * LANES)
                kl = k_l[rows, pl_]
                tn = []
                for hh in range(2):
                    h = 2 * p + hh
                    v = v_ref[b, rows, h * B_DV:(h + 1) * B_DV].astype(BF16)
                    qt = jnp.where(head_lanes[hh], q_t[rows, pl_], 0.0).astype(BF16)
                    att = lax.dot_general(qt, k_t[rows, pl_], NT, preferred_element_type=F32)
                    att = jnp.where(causal, att, 0.0).astype(BF16)
                    o_intra[c, h] = jnp.dot(att, v, preferred_element_type=F32)
                    tn.append(lax.dot_general(kl, v, TN, preferred_element_type=F32))
                upd[c, p] = jnp.where(top, tn[0], tn[1])

        for c in range(nc):
            rows = slice(c * C, (c + 1) * C)
            for p in range(B_HEADS // 2):
                pl_ = slice(p * LANES, (p + 1) * LANES)
                S = s_ref[b, p]
                S16 = S.astype(BF16)
                e_col = jnp.sum(jnp.where(eye, e_last[c * C:c * C + 1, pl_], 0.0), axis=-1, keepdims=True)
                s_ref[b, p] = e_col * S + upd[c, p]
                for hh in range(2):
                    h = 2 * p + hh
                    vs = slice(h * B_DV, (h + 1) * B_DV)
                    qi = jnp.where(head_lanes[hh], q_in[rows, pl_], 0.0).astype(BF16)
                    o = o_intra[c, h] + jnp.dot(qi, S16, preferred_element_type=F32)
                    mu = jnp.mean(o, axis=-1, keepdims=True)
                    d = o - mu
                    var = jnp.mean(d * d, axis=-1, keepdims=True)
                    rb = rb_ref[b, rows, vs]
                    o_ref[b, rows, vs] = (d * lax.rsqrt(var + EPS) * gn * (rb * jax.nn.sigmoid(rb))).astype(o_ref.dtype)

    @pl.when(t == pl.num_programs(1) - 1)
    def _():
        sf_ref[...] = s_ref[...]


def _gla(qb, kb, vb, la, rb, gn, s0, bb, tc):
    bsz, seq, _ = qb.shape
    chunk = int(np.gcd(seq, B_CHUNK))
    kspec = pl.BlockSpec((bb, tc, B_KW), lambda i, t: (i, t, 0))
    vspec = pl.BlockSpec((bb, tc, B_VW), lambda i, t: (i, t, 0))
    pair_shape = (B_HEADS // 2, 2 * B_DK, B_DV)
    sspec = pl.BlockSpec((bb,) + pair_shape, lambda i, t: (i, 0, 0, 0))
    o, s_fin = pl.pallas_call(
        functools.partial(_gla_body, bb=bb, tc=tc, chunk=chunk), grid=(bsz // bb, seq // tc),
        in_specs=[kspec, kspec, vspec, kspec, vspec, _const_spec(gn.shape), sspec],
        out_specs=[vspec, sspec],
        out_shape=[jax.ShapeDtypeStruct((bsz, seq, B_VW), BF16), jax.ShapeDtypeStruct((bsz,) + pair_shape, F32)],
        scratch_shapes=[pltpu.VMEM((bb,) + pair_shape, F32)],
        compiler_params=_cparams(("parallel", "arbitrary")), name="gla")(
            qb, kb, vb, la, rb, gn, s0.reshape((bsz,) + pair_shape))
    return o, s_fin.reshape(s0.shape)


def _mem_kv_body(m_ref, mn_ref, w_ref, knc_ref, o_ref):
    xn = (_rms(m_ref[...]) * mn_ref[...]).astype(BF16)
    kv = jnp.dot(xn, w_ref[...], preferred_element_type=F32)
    for h in range(C_HEADS):
        sl = slice(h * C_HEAD_DIM, (h + 1) * C_HEAD_DIM)
        o_ref[:, sl] = _rms(kv[:, sl]) * knc_ref[...]
    o_ref[:, C_WIDTH:] = kv[:, C_WIDTH:]


def _mem_kv(mem2d, mn, w, knc, tm):
    n = mem2d.shape[0]
    return pl.pallas_call(
        _mem_kv_body, grid=(n // tm,),
        in_specs=[pl.BlockSpec((tm, D_MODEL), lambda i: (i, 0)), _const_spec(mn.shape), _const_spec(w.shape),
                  _const_spec(knc.shape)],
        out_specs=pl.BlockSpec((tm, 2 * C_WIDTH), lambda i: (i, 0)),
        out_shape=jax.ShapeDtypeStruct((n, 2 * C_WIDTH), F32),
        compiler_params=_cparams(("parallel",)), name="mem_kv")(mem2d, mn, w, knc)


def _cross_body(q_ref, mem_ref, o_ref, *, bb, head_rows):
    for b in range(bb):
        q = q_ref[b].astype(BF16)
        for h in range(C_HEADS):
            sl = slice(h * C_HEAD_DIM, (h + 1) * C_HEAD_DIM)
            if head_rows:
                kh = mem_ref[b, pl.ds(h, N_MEM, stride=2 * C_HEADS), :].astype(BF16)
                vh = mem_ref[b, pl.ds(C_HEADS + h, N_MEM, stride=2 * C_HEADS), :].astype(BF16)
            else:
                kh = mem_ref[b, :, sl].astype(BF16)
                vh = mem_ref[b, :, C_WIDTH + h * C_HEAD_DIM:C_WIDTH + (h + 1) * C_HEAD_DIM].astype(BF16)
            s = lax.dot_general(q[:, sl], kh, NT, preferred_element_type=F32)
            p = jnp.exp(s - jnp.max(s, axis=-1, keepdims=True))
            l = jnp.sum(p, axis=-1, keepdims=True)
            o = jnp.dot(p.astype(BF16), vh, preferred_element_type=F32) / l
            o_ref[b, :, sl] = o.astype(o_ref.dtype)


def _cross(qc, mem_kv, bb, tq):
    bsz, seq, _ = qc.shape
    head_rows = mem_kv.shape[-1] == C_HEAD_DIM
    return pl.pallas_call(
        functools.partial(_cross_body, bb=bb, head_rows=head_rows), grid=(bsz // bb, seq // tq),
        in_specs=[pl.BlockSpec((bb, tq, C_WIDTH), lambda i, t: (i, t, 0)),
                  pl.BlockSpec((bb,) + mem_kv.shape[1:], lambda i, t: (i, 0, 0))],
        out_specs=pl.BlockSpec((bb, tq, C_WIDTH), lambda i, t: (i, t, 0)),
        out_shape=jax.ShapeDtypeStruct((bsz, seq, C_WIDTH), F32),
        compiler_params=_cparams(("parallel", "arbitrary")), name="cross")(qc, mem_kv)


def _merge_body(x_ref, nm_ref, wgl, oa_ref, ob_ref, oc_ref, wba, wbb, wbc, wout, nf_ref, wr, br,
                h_o, xt_o, route_o):
    x = x_ref[...]
    xn = (_rms(x) * nm_ref[...]).astype(BF16)
    merged = None
    for i, (o_ref, w_ref) in enumerate(((oa_ref, wba), (ob_ref, wbb), (oc_ref, wbc))):
        gate = jax.nn.sigmoid(jnp.dot(xn, wgl[:, i * D_MODEL:(i + 1) * D_MODEL], preferred_element_type=F32))
        term = gate * jnp.dot(o_ref[...].astype(BF16), w_ref[...], preferred_element_type=F32)
        merged = term if merged is None else merged + term
    h = x + jnp.dot(merged.astype(BF16), wout[...], preferred_element_type=F32)
    h_o[...] = h
    xt = _rms(h) * nf_ref[...]
    xt_o[...] = xt
    xt_hi = xt.astype(BF16)
    xt_lo = (xt - xt_hi.astype(F32)).astype(BF16)
    logits = (jnp.dot(xt_hi, wr[0], preferred_element_type=F32) + jnp.dot(xt_hi, wr[1], preferred_element_type=F32)
              + jnp.dot(xt_lo, wr[0], preferred_element_type=F32) + br[...])
    lane = lax.broadcasted_iota(jnp.int32, logits.shape, 1)
    lane_f = lane.astype(F32)
    big = float(LANES)
    is_g = (lane >= N_EXPERTS) & (lane < N_EXPERTS + N_GROUPS_E)
    glog = jnp.where(is_g, logits, NEG)
    gmax = jnp.max(glog, axis=-1, keepdims=True)
    grp = jnp.min(jnp.where(glog == gmax, lane_f, big), axis=-1, keepdims=True) - float(N_EXPERTS)
    g_w = 1.0 / jnp.sum(jnp.where(is_g, jnp.exp(glog - gmax), 0.0), axis=-1, keepdims=True)
    in_grp = (lane < N_EXPERTS) & (jnp.floor(lane_f * (1.0 / E_PER_GROUP)) == grp)
    elog = jnp.where(in_grp, logits, NEG)
    v1 = jnp.max(elog, axis=-1, keepdims=True)
    i1 = jnp.min(jnp.where(elog == v1, lane_f, big), axis=-1, keepdims=True)
    elog2 = jnp.where(lane_f == i1, NEG, elog)
    v2 = jnp.max(elog2, axis=-1, keepdims=True)
    i2 = jnp.min(jnp.where(elog2 == v2, lane_f, big), axis=-1, keepdims=True)
    e2 = jnp.exp(v2 - v1)
    w1 = g_w / (1.0 + e2)
    w2 = g_w * e2 / (1.0 + e2)
    route_o[...] = jnp.where(lane == 0, i1, jnp.where(lane == 1, i2, jnp.where(lane == 2, w1,
                             jnp.where(lane == 3, w2, 0.0))))


def _merge(x2d, oa, ob, oc, wts, tm):
    n = x2d.shape[0]
    consts = [wts['nm'], wts['wgl']]
    consts2 = [wts['wba'], wts['wbb'], wts['wbc'], wts['wout'], wts['nf'], wts['wr'], wts['br']]

    def tile(w):
        return pl.BlockSpec((tm, w), lambda i: (i, 0))

    in_specs = ([tile(D_MODEL)] + [_const_spec(a.shape) for a in consts]
                + [tile(A_GW), tile(B_VW), tile(C_WIDTH)] + [_const_spec(a.shape) for a in consts2])
    return pl.pallas_call(
        _merge_body, grid=(n // tm,), in_specs=in_specs,
        out_specs=[tile(D_MODEL), tile(D_MODEL), tile(LANES)],
        out_shape=[jax.ShapeDtypeStruct((n, D_MODEL), F32), jax.ShapeDtypeStruct((n, D_MODEL), F32),
                   jax.ShapeDtypeStruct((n, LANES), F32)],
        compiler_params=_cparams(("parallel",)), name="merge")(x2d, *consts, oa, ob, oc, *consts2)


SUBLANES = 8


def _for_rows(n_rows, fn):
    def body(g, carry):
        for r in range(SUBLANES):
            fn(g, r)
        return carry
    lax.fori_loop(0, n_rows // SUBLANES, body, 0)


def _wait_rows(hbm, rows, sem):
    pltpu.make_async_copy(hbm.at[pl.ds(0, rows)], hbm.at[pl.ds(0, rows)], sem).wait()


def _dispatch_body(*refs, tm, tiles, n_zero, zero_steps):
    dest_ref, zslot_ref = refs[0], refs[1]
    xt_refs = refs[2:2 + len(tiles)]
    xs_out, stage, zrow, sem = refs[2 + len(tiles):]
    j = pl.program_id(0)
    nsteps = pl.num_programs(0)
    slot = j % 2

    def wait_step(step, s):
        _wait_rows(xs_out, 2 * tm, sem.at[s])

        @pl.when(step < zero_steps)
        def _():
            _wait_rows(xs_out, n_zero, sem.at[s])

    @pl.when(j >= 2)
    def _():
        wait_step(j - 2, slot)

    @pl.when(j == 0)
    def _():
        zrow[...] = jnp.zeros_like(zrow)

    first = 0
    for xt_ref, nt in zip(xt_refs, tiles):
        @pl.when(jnp.logical_and(j >= first, j < first + nt))
        def _(xt_ref=xt_ref):
            stage[slot] = xt_ref[...].reshape(stage.shape[1:])
        first += nt

    for k in range(2):
        def start_row(g, r, k=k):
            pltpu.make_async_copy(stage.at[slot, g, pl.ds(r, 1)],
                                  xs_out.at[pl.ds(dest_ref[0, k * tm + g * SUBLANES + r], 1)], sem.at[slot]).start()
        _for_rows(tm, start_row)

    @pl.when(j < zero_steps)
    def _():
        def start_zero(g, r):
            pltpu.make_async_copy(zrow.at[pl.ds(0, 1)], xs_out.at[pl.ds(zslot_ref[0, g * SUBLANES + r], 1)],
                                  sem.at[slot]).start()
        _for_rows(n_zero, start_zero)

    @pl.when(j == nsteps - 1)
    def _():
        wait_step(j, slot)

        @pl.when(j >= 1)
        def _():
            wait_step(j - 1, 1 - slot)


def _dispatch(xts, dest3, unused, n_slots, tm):
    tiles = [xt.shape[0] // tm for xt in xts]
    nsteps = sum(tiles)
    zero_steps = tiles[0]
    assert unused.shape[0] % (zero_steps * SUBLANES) == 0
    n_zero = unused.shape[0] // zero_steps
    in_specs = [pl.BlockSpec((None, 1, 2 * tm), lambda j: (j, 0, 0), memory_space=pltpu.SMEM),
                pl.BlockSpec((None, 1, n_zero), lambda j: (jnp.minimum(j, zero_steps - 1), 0, 0),
                             memory_space=pltpu.SMEM)]
    first = 0
    for nt in tiles:
        in_specs.append(pl.BlockSpec((tm, D_MODEL), lambda j, first=first, nt=nt: (jnp.clip(j - first, 0, nt - 1), 0)))
        first += nt
    return pl.pallas_call(
        functools.partial(_dispatch_body, tm=tm, tiles=tiles, n_zero=n_zero, zero_steps=zero_steps), grid=(nsteps,),
        in_specs=in_specs,
        out_specs=pl.BlockSpec(memory_space=pl.ANY),
        out_shape=jax.ShapeDtypeStruct((n_slots, D_MODEL), F32),
        scratch_shapes=[pltpu.VMEM((2, tm // SUBLANES, SUBLANES, D_MODEL), F32),
                        pltpu.VMEM((SUBLANES, D_MODEL), F32), pltpu.SemaphoreType.DMA((2,))],
        compiler_params=_cparams(("arbitrary",)), name="dispatch")(
            dest3, unused.reshape(zero_steps, 1, n_zero), *xts)


def _expert_body(be_ref, xs_ref, wgu_ref, wd_ref, y_ref):
    del be_ref
    gu = jnp.dot(xs_ref[...].astype(BF16), wgu_ref[0], preferred_element_type=F32)
    gate = gu[:, :D_EXPERT]
    hid = gate * jax.nn.sigmoid(gate) * gu[:, D_EXPERT:]
    y_ref[...] = jnp.dot(hid.astype(BF16), wd_ref[0], preferred_element_type=F32)


def _experts(xs, block_e, wgu, wd, bm):
    nblk = block_e.shape[0]
    grid_spec = pltpu.PrefetchScalarGridSpec(
        num_scalar_prefetch=1, grid=(nblk,),
        in_specs=[
            pl.BlockSpec((bm, D_MODEL), lambda j, be: (j, 0)),
            pl.BlockSpec((1, D_MODEL, 2 * D_EXPERT), lambda j, be: (be[j], 0, 0)),
            pl.BlockSpec((1, D_EXPERT, D_MODEL), lambda j, be: (be[j], 0, 0)),
        ],
        out_specs=pl.BlockSpec((bm, D_MODEL), lambda j, be: (j, 0)))
    return pl.pallas_call(
        _expert_body, grid_spec=grid_spec,
        out_shape=jax.ShapeDtypeStruct((nblk * bm, D_MODEL), F32),
        compiler_params=_cparams(("parallel",)), name="experts")(block_e, xs, wgu, wd)


def _combine_body(d_cur, d_next, h_ref, route_ref, ys_hbm, y_ref, ybuf, sem, *, tm):
    j = pl.program_id(0)
    nblk = pl.num_programs(0)
    slot = j % 2

    def gather(idx_ref, s):
        def start_row(g, r):
            pltpu.make_async_copy(ys_hbm.at[pl.ds(idx_ref[0, g * SUBLANES + r], 1)], ybuf.at[s, g, pl.ds(r, 1)],
                                  sem.at[s]).start()
        _for_rows(2 * tm, start_row)

    @pl.when(j == 0)
    def _():
        gather(d_cur, 0)

    @pl.when(j + 1 < nblk)
    def _():
        gather(d_next, 1 - slot)

    _wait_rows(ys_hbm, 2 * tm, sem.at[slot])
    route = route_ref[...]
    w1 = route[:, 2:3]
    w2 = route[:, 3:4]
    ng = tm // SUBLANES
    y1 = ybuf[slot, 0:ng].reshape(tm, D_MODEL)
    y2 = ybuf[slot, ng:2 * ng].reshape(tm, D_MODEL)
    y_ref[...] = h_ref[...] + (y1 * w1 + y2 * w2)


def _tile_slots(dest, tm):
    nblk = dest.shape[0] // tm
    return dest.reshape(nblk, tm, 2).transpose(0, 2, 1).reshape(nblk, 1, 2 * tm)


def _combine(h, route, ys, dest, tm):
    n = h.shape[0]
    nblk = n // tm
    d3 = _tile_slots(dest, tm)
    return pl.pallas_call(
        functools.partial(_combine_body, tm=tm), grid=(nblk,),
        in_specs=[
            pl.BlockSpec((None, 1, 2 * tm), lambda j: (j, 0, 0), memory_space=pltpu.SMEM),
            pl.BlockSpec((None, 1, 2 * tm), lambda j: (jnp.minimum(j + 1, nblk - 1), 0, 0), memory_space=pltpu.SMEM),
            pl.BlockSpec((tm, D_MODEL), lambda j: (j, 0)),
            pl.BlockSpec((tm, LANES), lambda j: (j, 0)),
            pl.BlockSpec(memory_space=pl.ANY),
        ],
        out_specs=pl.BlockSpec((tm, D_MODEL), lambda j: (j, 0)),
        out_shape=jax.ShapeDtypeStruct((n, D_MODEL), F32),
        scratch_shapes=[pltpu.VMEM((2, 2 * tm // SUBLANES, SUBLANES, D_MODEL), F32), pltpu.SemaphoreType.DMA((2,))],
        compiler_params=_cparams(("arbitrary",)), name="combine")(d3, d3, h, route, ys)


def _dispatch_plan(route, bm):
    n = route.shape[0]
    flat_e = route[:, 0:2].astype(jnp.int32).reshape(-1)
    n_asg = flat_e.shape[0]
    onehot = (flat_e[:, None] == jnp.arange(N_EXPERTS, dtype=jnp.int32)[None, :]).astype(jnp.int32)
    csum = jnp.cumsum(onehot, axis=0)
    rank = jnp.sum(csum * onehot, axis=1) - 1
    counts = csum[-1]
    padded = (counts + bm - 1) // bm * bm
    pad_end = jnp.cumsum(padded)
    dest = jnp.sum((pad_end - padded)[None, :] * onehot, axis=1) + rank
    nblk = -(-n_asg // bm) + N_EXPERTS
    first_slot = jnp.arange(nblk, dtype=jnp.int32) * bm
    block_e = jnp.minimum(jnp.sum((pad_end[None, :] <= first_slot[:, None]).astype(jnp.int32), axis=1),
                          N_EXPERTS - 1)
    pad_cum = jnp.cumsum(padded - counts)
    i = jnp.arange(nblk * bm - n_asg, dtype=jnp.int32)
    seg = jnp.sum((pad_cum[None, :] <= i[:, None]).astype(jnp.int32), axis=1)
    seg_first_slot = jnp.concatenate([pad_end - padded + counts, pad_end[-1:]])
    seg_first_i = jnp.concatenate([jnp.zeros((1,), jnp.int32), pad_cum])
    unused = seg_first_slot[seg] + i - seg_first_i[seg]
    return dest.reshape(n, 2), block_e, unused


def _hmoe(groups, wts, bm, tm):
    route_all = jnp.concatenate([route for _, _, route in groups], axis=0)
    dest_all, block_e, unused = _dispatch_plan(route_all, bm)
    xs = _dispatch([xt for _, xt, _ in groups], _tile_slots(dest_all, tm), unused, block_e.shape[0] * bm, tm)
    ys = _experts(xs, block_e, wts['wgu'], wts['wd'], bm)
    ys_out, off = [], 0
    for h, _, route in groups:
        n = h.shape[0]
        ys_out.append(_combine(h, route, ys, dest_all[off:off + n], tm))
        off += n
    return ys_out


def _prep_weights(norm_mix, w_in, qn_a, kn_a, qn_c, kn_c, gla_gate_up, gla_gate_bias, gla_norm, mem_norm,
                  w_mem_kv, w_branch_a, w_branch_b, w_branch_c, w_out, norm_ffn, w_router_group, b_router_group,
                  w_router_expert, b_router_expert, w_exp_gate, w_exp_up, w_exp_down):
    splits = (A_WIDTH, A_WIDTH, A_WIDTH, B_KW, B_KW, B_VW, B_VW, B_RANK, C_WIDTH, 3 * D_MODEL)
    offs = np.concatenate([[0], np.cumsum(splits)])
    names = ('wqa', 'wka', 'wva', 'wqb', 'wkb', 'wvb', 'wrb', 'wab', 'wqc', 'wgl')
    wts = {nm: w_in[:, int(offs[i]):int(offs[i + 1])].astype(BF16) for i, nm in enumerate(names)}
    wts['wab'] = jnp.pad(wts['wab'], ((0, 0), (0, LANES - B_RANK)))
    wts['gup'] = jnp.pad(gla_gate_up.astype(F32), ((0, LANES - B_RANK), (0, 0)))
    wts['gbias'] = gla_gate_bias.reshape(1, B_KW)
    wts['nm'] = norm_mix.reshape(1, D_MODEL)
    wts['qna'] = jnp.tile(qn_a, 2).reshape(1, LANES)
    wts['kna'] = jnp.tile(kn_a, 2).reshape(1, LANES)
    wts['qnc'] = qn_c.reshape(1, C_HEAD_DIM)
    wts['knc'] = kn_c.reshape(1, C_HEAD_DIM)
    wts['gn'] = gla_norm.reshape(1, B_DV)
    wts['mn'] = mem_norm.reshape(1, D_MODEL)
    wts['wmem'] = w_mem_kv.astype(BF16)
    wts['wba'] = w_branch_a.astype(BF16)
    wts['wbb'] = w_branch_b.astype(BF16)
    wts['wbc'] = w_branch_c.astype(BF16)
    wts['wout'] = w_out.astype(BF16)
    wts['nf'] = norm_ffn.reshape(1, D_MODEL)
    wr = jnp.concatenate([w_router_expert, w_router_group], axis=1)
    br = jnp.concatenate([b_router_expert, b_router_group])
    npad = LANES - N_EXPERTS - N_GROUPS_E
    wr = jnp.pad(wr, ((0, 0), (0, npad)))
    wr_hi = wr.astype(BF16)
    wts['wr'] = jnp.stack([wr_hi, (wr - wr_hi.astype(F32)).astype(BF16)])
    wts['br'] = jnp.pad(br, (0, npad)).reshape(1, LANES)
    wts['wgu'] = jnp.concatenate([w_exp_gate, w_exp_up], axis=2).astype(BF16)
    wts['wd'] = w_exp_down.astype(BF16)
    return wts


def _tile(n, want):
    t = min(n, want)
    assert n % t == 0
    return t


def _layer(x, wts, attend, gla_state, mem_kv, cfg):
    bsz, seq, _ = x.shape
    n = bsz * seq
    x2d = x.reshape(n, D_MODEL)
    qa, ka, va, qb, kb, vb, rb, la, qc = _in_proj(x2d, wts, _tile(n, cfg['tm_in']))

    def r3(a):
        return a.reshape(bsz, seq, a.shape[-1])

    def r4(a):
        return a.reshape(a.shape[0], bsz, seq, LANES)

    oa, extras = attend(r4(qa), r4(ka), r4(va))
    ob, s_fin = _gla(r3(qb), r3(kb), r3(vb), r3(la), r3(rb), wts['gn'], gla_state, cfg['gla_bb'],
                     _tile(seq, cfg['gla_tc']))
    oc = _cross(r3(qc), mem_kv, cfg['cross_bb'], _tile(seq, cfg['cross_tq']))
    pre_ffn = _merge(x2d, oa.reshape(n, A_GW), ob.reshape(n, B_VW), oc.reshape(n, C_WIDTH), wts,
                     _tile(n, cfg['tm_merge']))
    return pre_ffn, extras, s_fin


def kernel(x_prompt, x_sample, mem_prompt, cache_win1, cache_win2, cache_win3, state_gla, cache_mem, norm_mix, w_in, qn_a, kn_a, qn_c, kn_c, gla_gate_up, gla_gate_bias, gla_norm, mem_norm, w_mem_kv, w_branch_a, w_branch_b, w_branch_c, w_out, norm_ffn, w_router_group, b_router_group, w_router_expert, b_router_expert, w_exp_gate, w_exp_up, w_exp_down):
    wts = _prep_weights(norm_mix, w_in, qn_a, kn_a, qn_c, kn_c, gla_gate_up, gla_gate_bias, gla_norm, mem_norm,
                        w_mem_kv, w_branch_a, w_branch_b, w_branch_c, w_out, norm_ffn, w_router_group,
                        b_router_group, w_router_expert, b_router_expert, w_exp_gate, w_exp_up, w_exp_down)
    bsz, seq, _ = x_prompt.shape
    bd, n_new, _ = x_sample.shape

    mem_p = _mem_kv(mem_prompt.reshape(bsz * N_MEM, D_MODEL), wts['mn'], wts['wmem'], wts['knc'], 256)
    mem_p = mem_p.reshape(bsz, N_MEM, 2 * C_WIDTH)

    def attend_prompt(qa, ka, va):
        oa = _dilated_prompt(qa, ka, va)
        bufs = []
        for gi, (w, _) in enumerate(A_GROUPS):
            wb = min(w, seq)
            kv = jnp.concatenate([a[2 * gi + c, :, seq - wb:] for a in (ka, va) for c in range(2)], axis=-1)
            bufs.append(kv.reshape(bsz, wb, 2, A_HEADS, A_HEAD_DIM))
        return oa, bufs

    cfg_p = dict(tm_in=256, tm_merge=512, gla_bb=2, gla_tc=256, cross_bb=1, cross_tq=512)
    s0 = jnp.zeros((bsz, B_HEADS, B_DK, B_DV), F32)
    ffn_in_p, wins_p, gla_p = _layer(x_prompt, wts, attend_prompt, s0, mem_p, cfg_p)

    def attend_sample(qa, ka, va):
        return _dilated_sample(qa, ka, va, (cache_win1, cache_win2, cache_win3), bb=1)

    cfg_s = dict(tm_in=256, tm_merge=256, gla_bb=8, gla_tc=n_new, cross_bb=8, cross_tq=n_new)
    ffn_in_s, wins_s, gla_s = _layer(x_sample, wts, attend_sample, state_gla,
                                     cache_mem.reshape(bd, N_MEM * 2 * C_HEADS, C_HEAD_DIM), cfg_s)

    y_p, y_s = _hmoe([ffn_in_p, ffn_in_s], wts, bm=256, tm=256)
    y_p = y_p.reshape(x_prompt.shape)
    y_s = y_s.reshape(x_sample.shape)

    return (y_p, y_s, wins_p[0], wins_p[1], wins_p[2], gla_p, mem_p.reshape(bsz, N_MEM, 2, C_HEADS, C_HEAD_DIM),
            wins_s[0], wins_s[1], wins_s[2], gla_s)
```

```python
import functools

import jax
import jax.numpy as jnp
import numpy as np
from jax import lax
from jax.experimental import pallas as pl
from jax.experimental.pallas import tpu as pltpu

F32 = jnp.float32
BF16 = jnp.bfloat16
HIGHEST = lax.Precision.HIGHEST

D_MODEL = 1024
A_GROUPS = ((128, 1), (512, 4), (2048, 16))
A_J = 128
A_HEADS = 4
A_HEAD_DIM = 64
A_GW = A_HEADS * A_HEAD_DIM
A_WIDTH = 3 * A_GW
B_HEADS = 4
B_DK = 64
B_DV = 128
B_KW = B_HEADS * B_DK
B_VW = B_HEADS * B_DV
B_RANK = 16
B_TAU = 16.0
B_CHUNK = 64
C_HEADS = 4
C_HEAD_DIM = 128
C_WIDTH = C_HEADS * C_HEAD_DIM
N_MEM = 256
N_GROUPS_E = 4
E_PER_GROUP = 8
N_EXPERTS = 32
D_EXPERT = 256
EPS = 1e-6
NEG = -1e30
LOG2E = 1.4426950408889634
LANES = 128
VMEM_LIMIT = 56 * 1024 * 1024

NT = (((1,), (1,)), ((), ()))
TN = (((0,), (0,)), ((), ()))


def _cparams(sem):
    return pltpu.CompilerParams(dimension_semantics=sem, vmem_limit_bytes=VMEM_LIMIT)


def _const_spec(shape, single=True):
    nd = len(shape)
    if single:
        return pl.BlockSpec(shape, lambda *_: (0,) * nd, pipeline_mode=pl.Buffered(1))
    return pl.BlockSpec(shape, lambda *_: (0,) * nd)


def _rms(x):
    return x * lax.rsqrt(jnp.mean(x * x, axis=-1, keepdims=True) + EPS)


def _headnorm64(h, g128, scale, o_ref):
    lo = lax.broadcasted_iota(jnp.int32, (1, LANES), 1) < A_HEAD_DIM
    for c in range(h.shape[1] // LANES):
        x = h[:, c * LANES:(c + 1) * LANES]
        x2 = x * x
        s_lo = jnp.sum(jnp.where(lo, x2, 0.0), axis=-1, keepdims=True)
        s_hi = jnp.sum(jnp.where(lo, 0.0, x2), axis=-1, keepdims=True)
        ms = jnp.where(lo, s_lo, s_hi) * (1.0 / A_HEAD_DIM)
        y = x * lax.rsqrt(ms + EPS) * g128
        if scale != 1.0:
            y = y * scale
        o_ref[c] = y.astype(o_ref.dtype)


def _in_proj_body(x_ref, nm_ref, wqa, wka, wva, wqb, wkb, wvb, wrb, wab, wqc, qna, kna, qnc, gup, gbias,
                  qa_o, ka_o, va_o, qb_o, kb_o, vb_o, rb_o, la_o, qc_o):
    xn = (_rms(x_ref[...]) * nm_ref[...]).astype(BF16)

    def mm(w):
        return jnp.dot(xn, w[...], preferred_element_type=F32)

    _headnorm64(mm(wqa), qna[...], A_HEAD_DIM ** -0.5, qa_o)
    _headnorm64(mm(wka), kna[...], 1.0, ka_o)
    hv = mm(wva)
    for c in range(A_WIDTH // LANES):
        va_o[c] = hv[:, c * LANES:(c + 1) * LANES]
    qb_o[...] = mm(wqb) * (B_DK ** -0.5)
    kb_o[...] = mm(wkb)
    vb_o[...] = mm(wvb)
    rb_o[...] = mm(wrb)
    z = jnp.dot(mm(wab), gup[...], precision=HIGHEST, preferred_element_type=F32) + gbias[...]
    la_o[...] = (jnp.minimum(z, 0.0) - jnp.log1p(jnp.exp(-jnp.abs(z)))) * (1.0 / B_TAU)
    hq = mm(wqc)
    for h in range(C_HEADS):
        sl = slice(h * C_HEAD_DIM, (h + 1) * C_HEAD_DIM)
        qc_o[:, sl] = _rms(hq[:, sl]) * qnc[...] * (C_HEAD_DIM ** -0.5)


def _in_proj(x2d, wts, tm):
    n = x2d.shape[0]
    ins = [x2d, wts['nm'], wts['wqa'], wts['wka'], wts['wva'], wts['wqb'], wts['wkb'], wts['wvb'], wts['wrb'],
           wts['wab'], wts['wqc'], wts['qna'], wts['kna'], wts['qnc'], wts['gup'], wts['gbias']]
    in_specs = [pl.BlockSpec((tm, D_MODEL), lambda i: (i, 0))] + [_const_spec(a.shape, single=False) for a in ins[1:]]
    nch = A_WIDTH // LANES
    widths = [B_KW, B_KW, B_VW, B_VW, B_KW, C_WIDTH]
    out_shape = ([jax.ShapeDtypeStruct((nch, n, LANES), F32)] * 3
                 + [jax.ShapeDtypeStruct((n, w), F32) for w in widths])
    out_specs = ([pl.BlockSpec((nch, tm, LANES), lambda i: (0, i, 0))] * 3
                 + [pl.BlockSpec((tm, w), lambda i: (i, 0)) for w in widths])
    return pl.pallas_call(
        _in_proj_body, grid=(n // tm,), in_specs=in_specs, out_specs=out_specs, out_shape=out_shape,
        compiler_params=_cparams(("parallel",)), name="in_proj")(*ins)


def _dil_prompt_body(q_ref, k_ref, v_ref, o_ref, os_ref, ms_ref, ls_ref, *, seq):
    g = pl.program_id(1)
    J = A_J
    row = lax.broadcasted_iota(jnp.int32, (J, J), 0)
    col = lax.broadcasted_iota(jnp.int32, (J, J), 1)
    cur_ok = col <= row
    lo = lax.broadcasted_iota(jnp.int32, (1, LANES), 1) < A_HEAD_DIM
    hi = jnp.logical_not(lo)

    for gi, (_, r) in enumerate(A_GROUPS):
        nb = seq // (r * J)

        @pl.when(g == gi)
        def _(gi=gi, r=r, nb=nb):
            def blk(i, carry):
                rho = i >> (nb.bit_length() - 1)
                n = i & (nb - 1)
                start = rho + n * (r * J)
                pstart = jnp.maximum(start - r * J, 0)
                if r == 1:
                    start, pstart = pl.multiple_of(start, J), pl.multiple_of(pstart, J)
                prev_ok = col >= row + jnp.where(n > 0, 0, J)

                def ld(ref, c, s):
                    return ref[c, 0, pl.ds(s, J, stride=r), :]

                heads = []
                for c in range(2):
                    q = ld(q_ref, c, start) * LOG2E
                    kc, vc = ld(k_ref, c, start).astype(BF16), ld(v_ref, c, start).astype(BF16)
                    kp, vp = ld(k_ref, c, pstart).astype(BF16), ld(v_ref, c, pstart).astype(BF16)
                    for lanes_ok in (lo, hi):
                        qh = jnp.where(lanes_ok, q, 0.0).astype(BF16)
                        s_c = lax.dot_general(qh, kc, NT, preferred_element_type=F32)
                        s_p = lax.dot_general(qh, kp, NT, preferred_element_type=F32)
                        heads.append((s_c, s_p, vc, vp))
                outs = []
                for s_c, s_p, vc, vp in heads:
                    s_c = jnp.where(cur_ok, s_c, NEG)
                    s_p = jnp.where(prev_ok, s_p, NEG)
                    m = jnp.max(jnp.maximum(s_c, s_p), axis=-1, keepdims=True)
                    p_c = jnp.exp2(s_c - m)
                    p_p = jnp.exp2(s_p - m)
                    l = jnp.sum(p_c + p_p, axis=-1, keepdims=True)
                    pv = (jnp.dot(p_c.astype(BF16), vc, preferred_element_type=F32)
                          + jnp.dot(p_p.astype(BF16), vp, preferred_element_type=F32))
                    outs.append((pv, m, l))
                for c in range(2):
                    (o0, m0, l0), (o1, m1, l1) = outs[2 * c], outs[2 * c + 1]
                    os_ref[gi, c, pl.ds(start, J, stride=r), :] = jnp.where(lo, o0, o1)
                    ms_ref[gi, c, pl.ds(start, J, stride=r), :] = jnp.where(lo, m0, m1)
                    ls_ref[gi, c, pl.ds(start, J, stride=r), :] = jnp.where(lo, l0, l1)
                return carry

            lax.fori_loop(0, seq // J, blk, 0, unroll=2)

    @pl.when(g == len(A_GROUPS) - 1)
    def _():
        def comb(i, carry):
            sl = pl.ds(pl.multiple_of(i * J, J), J)
            for c in range(2):
                m0, m1, m2 = ms_ref[0, c, sl, :], ms_ref[1, c, sl, :], ms_ref[2, c, sl, :]
                mx = jnp.maximum(jnp.maximum(m0, m1), m2)
                w0, w1, w2 = jnp.exp2(m0 - mx), jnp.exp2(m1 - mx), jnp.exp2(m2 - mx)
                num = w0 * os_ref[0, c, sl, :] + w1 * os_ref[1, c, sl, :] + w2 * os_ref[2, c, sl, :]
                den = w0 * ls_ref[0, c, sl, :] + w1 * ls_ref[1, c, sl, :] + w2 * ls_ref[2, c, sl, :]
                o_ref[0, sl, c * LANES:(c + 1) * LANES] = (num / den).astype(o_ref.dtype)
            return carry

        lax.fori_loop(0, seq // J, comb, 0)


def _dilated_prompt(qa, ka, va):
    _, bsz, seq, _ = qa.shape
    ng = len(A_GROUPS)
    spec = pl.BlockSpec((2, 1, seq, LANES), lambda b, g: (g, b, 0, 0))
    return pl.pallas_call(
        functools.partial(_dil_prompt_body, seq=seq), grid=(bsz, ng),
        in_specs=[spec, spec, spec],
        out_specs=pl.BlockSpec((1, seq, A_GW), lambda b, g: (b, 0, 0)),
        out_shape=jax.ShapeDtypeStruct((bsz, seq, A_GW), BF16),
        scratch_shapes=[pltpu.VMEM((ng, 2, seq, LANES), F32)] * 3,
        compiler_params=_cparams(("parallel", "arbitrary")), name="dilated_prompt")(qa, ka, va)


def _dil_sample_body(q_ref, k_ref, v_ref, c1_ref, c2_ref, c3_ref, o_ref, n1_ref, n2_ref, n3_ref, *, bb, n_new):
    L = n_new
    caches = (c1_ref, c2_ref, c3_ref)
    outs = (n1_ref, n2_ref, n3_ref)
    nq = A_HEADS * L
    head_of_lane = lax.broadcasted_iota(jnp.int32, (1, A_GW), 1) >> (A_HEAD_DIM.bit_length() - 1)
    q_of_row = lax.broadcasted_iota(jnp.int32, (nq, 1), 0) & (L - 1)
    t_new = lax.broadcasted_iota(jnp.int32, (nq, L), 1)
    lane = lax.broadcasted_iota(jnp.int32, (1, LANES), 1)
    pad_rows = jnp.zeros((LANES - L, 2 * A_GW), F32)

    def per_head_rows(x):
        return jnp.concatenate([jnp.where(head_of_lane == h, x, 0.0) for h in range(A_HEADS)], axis=0)

    for b in range(bb):
        scores, scores_n, vts, vns = [], [], [], []
        for gi, (w, r) in enumerate(A_GROUPS):
            cref = caches[gi]
            qg, kn, vn = [jnp.concatenate([ref[2 * gi, b], ref[2 * gi + 1, b]], axis=-1)
                          for ref in (q_ref, k_ref, v_ref)]
            qbd = per_head_rows(qg).astype(BF16)
            kt = cref[b, 0:A_GW, :].astype(BF16)
            s = jnp.dot(qbd, kt, preferred_element_type=F32)
            s_n = lax.dot_general(qbd, kn.astype(BF16), NT, preferred_element_type=F32)
            pos = lax.broadcasted_iota(jnp.int32, (nq, w), 1)
            s = jnp.where((pos >= q_of_row) & (((q_of_row - pos) & (r - 1)) == 0), s, NEG)
            s_n = jnp.where((t_new <= q_of_row) & (((q_of_row - t_new) & (r - 1)) == 0), s_n, NEG)
            scores.append(s)
            scores_n.append(s_n)
            vts.append(cref[b, A_GW:2 * A_GW, :].astype(BF16))
            vns.append(vn.astype(BF16))

            rolled = pltpu.roll(cref[b], w - L, axis=1)
            outs[gi][b] = rolled
            new_t = jnp.concatenate([jnp.concatenate([kn, vn], axis=-1), pad_rows], axis=0).T
            outs[gi][b, :, w - LANES:w] = jnp.where(lane < LANES - L, rolled[:, w - LANES:w],
                                                    pltpu.roll(new_t, LANES - L, axis=1))

        m = None
        for s in scores + scores_n:
            ms = jnp.max(s, axis=-1, keepdims=True)
            m = ms if m is None else jnp.maximum(m, ms)
        den = 0.0
        acc = jnp.zeros((nq, A_GW), F32)
        for s, s_n, vt, vn in zip(scores, scores_n, vts, vns):
            p = jnp.exp(s - m)
            p_n = jnp.exp(s_n - m)
            den = den + jnp.sum(p, axis=-1, keepdims=True) + jnp.sum(p_n, axis=-1, keepdims=True)
            acc = acc + lax.dot_general(p.astype(BF16), vt, NT, preferred_element_type=F32)
            acc = acc + jnp.dot(p_n.astype(BF16), vn, preferred_element_type=F32)
        acc = acc / den
        o = jnp.zeros((L, A_GW), F32)
        for h in range(A_HEADS):
            o = o + jnp.where(head_of_lane == h, acc[h * L:(h + 1) * L, :], 0.0)
        o_ref[b] = o.astype(o_ref.dtype)


def _dilated_sample(qa, ka, va, caches, bb):
    _, bd, n_new, _ = qa.shape
    cts = [c.transpose(0, 2, 3, 4, 1).reshape(bd, 2 * A_GW, c.shape[1]) for c in caches]
    new_spec = pl.BlockSpec((A_WIDTH // LANES, bb, n_new, LANES), lambda i: (0, i, 0, 0))
    cspecs = [pl.BlockSpec((bb,) + c.shape[1:], lambda i: (i, 0, 0)) for c in cts]
    oa, *nts = pl.pallas_call(
        functools.partial(_dil_sample_body, bb=bb, n_new=n_new), grid=(bd // bb,),
        in_specs=[new_spec, new_spec, new_spec] + cspecs,
        out_specs=[pl.BlockSpec((bb, n_new, A_GW), lambda i: (i, 0, 0))] + cspecs,
        out_shape=[jax.ShapeDtypeStruct((bd, n_new, A_GW), F32)] + [jax.ShapeDtypeStruct(c.shape, F32) for c in cts],
        compiler_params=_cparams(("parallel",)), name="dilated_sample")(qa, ka, va, *cts)
    news = [n.reshape(bd, 2, A_HEADS, A_HEAD_DIM, n.shape[-1]).transpose(0, 4, 1, 2, 3) for n in nts]
    return oa, news


def _gla_body(q_ref, k_ref, v_ref, la_ref, rb_ref, gn_ref, s0_ref, o_ref, sf_ref, s_ref, *, bb, tc, chunk):
    t = pl.program_id(1)
    C = chunk

    @pl.when(t == 0)
    def _():
        s_ref[...] = s0_ref[...]

    causal = lax.broadcasted_iota(jnp.int32, (C, C), 0) >= lax.broadcasted_iota(jnp.int32, (C, C), 1)
    lo = lax.broadcasted_iota(jnp.int32, (1, LANES), 1) < B_DK
    head_lanes = (lo, jnp.logical_not(lo))
    row_s = lax.broadcasted_iota(jnp.int32, (LANES, LANES), 0)
    eye = row_s == lax.broadcasted_iota(jnp.int32, (LANES, LANES), 1)
    top = row_s < B_DK
    pos = lax.broadcasted_iota(jnp.int32, (tc, 1), 0) & (C - 1)
    mid = C // 2 - 1
    gn = gn_ref[...]
    nc = tc // C

    def per_chunk_row(x, r):
        return jnp.concatenate([jnp.broadcast_to(x[c * C + r:c * C + r + 1, :], (C, x.shape[1])) for c in range(nc)],
                               axis=0)

    for b in range(bb):
        cum = la_ref[b]
        step = 1
        while step < C:
            cum = cum + jnp.where(pos >= step, pltpu.roll(cum, step, axis=0), 0.0)
            step *= 2
        cmid = per_chunk_row(cum, mid)
        clast = per_chunk_row(cum, C - 1)
        q = q_ref[b]
        k = k_ref[b]
        q_in = q * jnp.exp(cum)
        q_t = q * jnp.exp(cum - cmid)
        k_t = (k * jnp.exp(cmid - cum)).astype(BF16)
        k_l = (k * jnp.exp(clast - cum)).astype(BF16)
        e_last = jnp.exp(clast)

        o_intra = {}
        upd = {}
        for c in range(nc):
            rows = slice(c * C, (c + 1) * C)
            for p in range(B_HEADS // 2):
                pl_ = slice(p * LANES, (p + 1) * LANES)
                kl = k_l[rows, pl_]
                tn = []
                for hh in range(2):
                    h = 2 * p + hh
                    v = v_ref[b, rows, h * B_DV:(h + 1) * B_DV].astype(BF16)
                    qt = jnp.where(head_lanes[hh], q_t[rows, pl_], 0.0).astype(BF16)
                    att = lax.dot_general(qt, k_t[rows, pl_], NT, preferred_element_type=F32)
                    att = jnp.where(causal, att, 0.0).astype(BF16)
                    o_intra[c, h] = jnp.dot(att, v, preferred_element_type=F32)
                    tn.append(lax.dot_general(kl, v, TN, preferred_element_type=F32))
                upd[c, p] = jnp.where(top, tn[0], tn[1])

        for c in range(nc):
            rows = slice(c * C, (c + 1) * C)
            for p in range(B_HEADS // 2):
                pl_ = slice(p * LANES, (p + 1) * LANES)
                S = s_ref[b, p]
                S16 = S.astype(BF16)
                e_col = jnp.sum(jnp.where(eye, e_last[c * C:c * C + 1, pl_], 0.0), axis=-1, keepdims=True)
                s_ref[b, p] = e_col * S + upd[c, p]
                for hh in range(2):
                    h = 2 * p + hh
                    vs = slice(h * B_DV, (h + 1) * B_DV)
                    qi = jnp.where(head_lanes[hh], q_in[rows, pl_], 0.0).astype(BF16)
                    o = o_intra[c, h] + jnp.dot(qi, S16, preferred_element_type=F32)
                    mu = jnp.mean(o, axis=-1, keepdims=True)
                    d = o - mu
                    var = jnp.mean(d * d, axis=-1, keepdims=True)
                    rb = rb_ref[b, rows, vs]
                    o_ref[b, rows, vs] = (d * lax.rsqrt(var + EPS) * gn * (rb * jax.nn.sigmoid(rb))).astype(o_ref.dtype)

    @pl.when(t == pl.num_programs(1) - 1)
    def _():
        sf_ref[...] = s_ref[...]


def _gla(qb, kb, vb, la, rb, gn, s0, bb, tc):
    bsz, seq, _ = qb.shape
    chunk = int(np.gcd(seq, B_CHUNK))
    kspec = pl.BlockSpec((bb, tc, B_KW), lambda i, t: (i, t, 0))
    vspec = pl.BlockSpec((bb, tc, B_VW), lambda i, t: (i, t, 0))
    pair_shape = (B_HEADS // 2, 2 * B_DK, B_DV)
    sspec = pl.BlockSpec((bb,) + pair_shape, lambda i, t: (i, 0, 0, 0))
    o, s_fin = pl.pallas_call(
        functools.partial(_gla_body, bb=bb, tc=tc, chunk=chunk), grid=(bsz // bb, seq // tc),
        in_specs=[kspec, kspec, vspec, kspec, vspec, _const_spec(gn.shape), sspec],
        out_specs=[vspec, sspec],
        out_shape=[jax.ShapeDtypeStruct((bsz, seq, B_VW), BF16), jax.ShapeDtypeStruct((bsz,) + pair_shape, F32)],
        scratch_shapes=[pltpu.VMEM((bb,) + pair_shape, F32)],
        compiler_params=_cparams(("parallel", "arbitrary")), name="gla")(
            qb, kb, vb, la, rb, gn, s0.reshape((bsz,) + pair_shape))
    return o, s_fin.reshape(s0.shape)


def _mem_kv_body(m_ref, mn_ref, w_ref, knc_ref, o_ref):
    xn = (_rms(m_ref[...]) * mn_ref[...]).astype(BF16)
    kv = jnp.dot(xn, w_ref[...], preferred_element_type=F32)
    tm = kv.shape[0]
    nrow = 2 * C_HEADS
    for j in range(nrow):
        x = kv[:, j * C_HEAD_DIM:(j + 1) * C_HEAD_DIM]
        if j < C_HEADS:
            x = _rms(x) * knc_ref[...]
        o_ref[pl.ds(j, tm, stride=nrow), :] = x


def _mem_kv(mem2d, mn, w, knc, tm):
    n = mem2d.shape[0]
    nrow = 2 * C_HEADS
    return pl.pallas_call(
        _mem_kv_body, grid=(n // tm,),
        in_specs=[pl.BlockSpec((tm, D_MODEL), lambda i: (i, 0)), _const_spec(mn.shape), _const_spec(w.shape),
                  _const_spec(knc.shape)],
        out_specs=pl.BlockSpec((tm * nrow, C_HEAD_DIM), lambda i: (i, 0)),
        out_shape=jax.ShapeDtypeStruct((n * nrow, C_HEAD_DIM), F32),
        compiler_params=_cparams(("parallel",)), name="mem_kv")(mem2d, mn, w, knc)


def _cross_body(q_ref, mem_ref, o_ref, *, bb):
    for b in range(bb):
        q = q_ref[b].astype(BF16)
        for h in range(C_HEADS):
            sl = slice(h * C_HEAD_DIM, (h + 1) * C_HEAD_DIM)
            kh = mem_ref[b, pl.ds(h, N_MEM, stride=2 * C_HEADS), :].astype(BF16)
            vh = mem_ref[b, pl.ds(C_HEADS + h, N_MEM, stride=2 * C_HEADS), :].astype(BF16)
            s = lax.dot_general(q[:, sl], kh, NT, preferred_element_type=F32)
            p = jnp.exp(s - jnp.max(s, axis=-1, keepdims=True))
            l = jnp.sum(p, axis=-1, keepdims=True)
            o = jnp.dot(p.astype(BF16), vh, preferred_element_type=F32) / l
            o_ref[b, :, sl] = o.astype(o_ref.dtype)


def _cross(qc, mem_kv, bb, tq):
    bsz, seq, _ = qc.shape
    return pl.pallas_call(
        functools.partial(_cross_body, bb=bb), grid=(bsz // bb, seq // tq),
        in_specs=[pl.BlockSpec((bb, tq, C_WIDTH), lambda i, t: (i, t, 0)),
                  pl.BlockSpec((bb,) + mem_kv.shape[1:], lambda i, t: (i, 0, 0))],
        out_specs=pl.BlockSpec((bb, tq, C_WIDTH), lambda i, t: (i, t, 0)),
        out_shape=jax.ShapeDtypeStruct((bsz, seq, C_WIDTH), F32),
        compiler_params=_cparams(("parallel", "arbitrary")), name="cross")(qc, mem_kv)


def _merge_body(x_ref, nm_ref, wgl, oa_ref, ob_ref, oc_ref, wba, wbb, wbc, wout, nf_ref, wr, br,
                h_o, xt_o, route_o):
    x = x_ref[...]
    xn = (_rms(x) * nm_ref[...]).astype(BF16)
    merged = None
    for i, (o_ref, w_ref) in enumerate(((oa_ref, wba), (ob_ref, wbb), (oc_ref, wbc))):
        gate = jax.nn.sigmoid(jnp.dot(xn, wgl[:, i * D_MODEL:(i + 1) * D_MODEL], preferred_element_type=F32))
        term = gate * jnp.dot(o_ref[...].astype(BF16), w_ref[...], preferred_element_type=F32)
        merged = term if merged is None else merged + term
    h = x + jnp.dot(merged.astype(BF16), wout[...], preferred_element_type=F32)
    h_o[...] = h
    xt = _rms(h) * nf_ref[...]
    xt_o[...] = xt
    xt_hi = xt.astype(BF16)
    xt_lo = (xt - xt_hi.astype(F32)).astype(BF16)
    logits = (jnp.dot(xt_hi, wr[0], preferred_element_type=F32) + jnp.dot(xt_hi, wr[1], preferred_element_type=F32)
              + jnp.dot(xt_lo, wr[0], preferred_element_type=F32) + br[...])
    lane = lax.broadcasted_iota(jnp.int32, logits.shape, 1)
    lane_f = lane.astype(F32)
    big = float(LANES)
    is_g = (lane >= N_EXPERTS) & (lane < N_EXPERTS + N_GROUPS_E)
    glog = jnp.where(is_g, logits, NEG)
    gmax = jnp.max(glog, axis=-1, keepdims=True)
    grp = jnp.min(jnp.where(glog == gmax, lane_f, big), axis=-1, keepdims=True) - float(N_EXPERTS)
    g_w = 1.0 / jnp.sum(jnp.where(is_g, jnp.exp(glog - gmax), 0.0), axis=-1, keepdims=True)
    in_grp = (lane < N_EXPERTS) & (jnp.floor(lane_f * (1.0 / E_PER_GROUP)) == grp)
    elog = jnp.where(in_grp, logits, NEG)
    v1 = jnp.max(elog, axis=-1, keepdims=True)
    i1 = jnp.min(jnp.where(elog == v1, lane_f, big), axis=-1, keepdims=True)
    elog2 = jnp.where(lane_f == i1, NEG, elog)
    v2 = jnp.max(elog2, axis=-1, keepdims=True)
    i2 = jnp.min(jnp.where(elog2 == v2, lane_f, big), axis=-1, keepdims=True)
    e2 = jnp.exp(v2 - v1)
    w1 = g_w / (1.0 + e2)
    w2 = g_w * e2 / (1.0 + e2)
    route_o[...] = jnp.where(lane == 0, i1, jnp.where(lane == 1, i2, jnp.where(lane == 2, w1,
                             jnp.where(lane == 3, w2, 0.0))))


def _merge(x2d, oa, ob, oc, wts, tm):
    n = x2d.shape[0]
    consts = [wts['nm'], wts['wgl']]
    consts2 = [wts['wba'], wts['wbb'], wts['wbc'], wts['wout'], wts['nf'], wts['wr'], wts['br']]

    def tile(w):
        return pl.BlockSpec((tm, w), lambda i: (i, 0))

    in_specs = ([tile(D_MODEL)] + [_const_spec(a.shape) for a in consts]
                + [tile(A_GW), tile(B_VW), tile(C_WIDTH)] + [_const_spec(a.shape) for a in consts2])
    return pl.pallas_call(
        _merge_body, grid=(n // tm,), in_specs=in_specs,
        out_specs=[tile(D_MODEL), tile(D_MODEL), tile(LANES)],
        out_shape=[jax.ShapeDtypeStruct((n, D_MODEL), F32), jax.ShapeDtypeStruct((n, D_MODEL), F32),
                   jax.ShapeDtypeStruct((n, LANES), F32)],
        compiler_params=_cparams(("parallel",)), name="merge")(x2d, *consts, oa, ob, oc, *consts2)


SUBLANES = 8


def _for_rows(n_rows, fn):
    def body(g, carry):
        for r in range(SUBLANES):
            fn(g, r)
        return carry
    lax.fori_loop(0, n_rows // SUBLANES, body, 0)


def _wait_rows(hbm, rows, sem):
    pltpu.make_async_copy(hbm.at[pl.ds(0, rows)], hbm.at[pl.ds(0, rows)], sem).wait()


def _dispatch_body(*refs, tm, tiles, n_zero, zero_steps):
    dest_ref, zslot_ref = refs[0], refs[1]
    xt_refs = refs[2:2 + len(tiles)]
    xs_out, stage, zrow, sem = refs[2 + len(tiles):]
    j = pl.program_id(0)
    nsteps = pl.num_programs(0)
    slot = j % 2

    def wait_step(step, s):
        _wait_rows(xs_out, 2 * tm, sem.at[s])

        @pl.when(step < zero_steps)
        def _():
            _wait_rows(xs_out, n_zero, sem.at[s])

    @pl.when(j >= 2)
    def _():
        wait_step(j - 2, slot)

    @pl.when(j == 0)
    def _():
        zrow[...] = jnp.zeros_like(zrow)

    first = 0
    for xt_ref, nt in zip(xt_refs, tiles):
        @pl.when(jnp.logical_and(j >= first, j < first + nt))
        def _(xt_ref=xt_ref):
            stage[slot] = xt_ref[...].reshape(stage.shape[1:])
        first += nt

    for k in range(2):
        def start_row(g, r, k=k):
            pltpu.make_async_copy(stage.at[slot, g, pl.ds(r, 1)],
                                  xs_out.at[pl.ds(dest_ref[0, k * tm + g * SUBLANES + r], 1)], sem.at[slot]).start()
        _for_rows(tm, start_row)

    @pl.when(j < zero_steps)
    def _():
        def start_zero(g, r):
            pltpu.make_async_copy(zrow.at[pl.ds(0, 1)], xs_out.at[pl.ds(zslot_ref[0, g * SUBLANES + r], 1)],
                                  sem.at[slot]).start()
        _for_rows(n_zero, start_zero)

    @pl.when(j == nsteps - 1)
    def _():
        wait_step(j, slot)

        @pl.when(j >= 1)
        def _():
            wait_step(j - 1, 1 - slot)


def _dispatch(xts, dest3, unused, n_slots, tm):
    tiles = [xt.shape[0] // tm for xt in xts]
    nsteps = sum(tiles)
    zero_steps = tiles[0]
    assert unused.shape[0] % (zero_steps * SUBLANES) == 0
    n_zero = unused.shape[0] // zero_steps
    in_specs = [pl.BlockSpec((None, 1, 2 * tm), lambda j: (j, 0, 0), memory_space=pltpu.SMEM),
                pl.BlockSpec((None, 1, n_zero), lambda j: (jnp.minimum(j, zero_steps - 1), 0, 0),
                             memory_space=pltpu.SMEM)]
    first = 0
    for nt in tiles:
        in_specs.append(pl.BlockSpec((tm, D_MODEL), lambda j, first=first, nt=nt: (jnp.clip(j - first, 0, nt - 1), 0)))
        first += nt
    return pl.pallas_call(
        functools.partial(_dispatch_body, tm=tm, tiles=tiles, n_zero=n_zero, zero_steps=zero_steps), grid=(nsteps,),
        in_specs=in_specs,
        out_specs=pl.BlockSpec(memory_space=pl.ANY),
        out_shape=jax.ShapeDtypeStruct((n_slots, D_MODEL), F32),
        scratch_shapes=[pltpu.VMEM((2, tm // SUBLANES, SUBLANES, D_MODEL), F32),
                        pltpu.VMEM((SUBLANES, D_MODEL), F32), pltpu.SemaphoreType.DMA((2,))],
        compiler_params=_cparams(("arbitrary",)), name="dispatch")(
            dest3, unused.reshape(zero_steps, 1, n_zero), *xts)


def _expert_body(be_ref, xs_ref, wg_ref, wu_ref, wd_ref, y_ref):
    del be_ref
    xb = xs_ref[...].astype(BF16)
    gate = jnp.dot(xb, wg_ref[0], preferred_element_type=F32)
    hid = gate * jax.nn.sigmoid(gate) * jnp.dot(xb, wu_ref[0], preferred_element_type=F32)
    y_ref[...] = jnp.dot(hid.astype(BF16), wd_ref[0], preferred_element_type=F32)


def _experts(xs, block_e, wg, wu, wd, bm):
    nblk = block_e.shape[0]
    up_spec = pl.BlockSpec((1, D_MODEL, D_EXPERT), lambda j, be: (be[j], 0, 0))
    grid_spec = pltpu.PrefetchScalarGridSpec(
        num_scalar_prefetch=1, grid=(nblk,),
        in_specs=[pl.BlockSpec((bm, D_MODEL), lambda j, be: (j, 0)), up_spec, up_spec,
                  pl.BlockSpec((1, D_EXPERT, D_MODEL), lambda j, be: (be[j], 0, 0))],
        out_specs=pl.BlockSpec((bm, D_MODEL), lambda j, be: (j, 0)))
    return pl.pallas_call(
        _expert_body, grid_spec=grid_spec,
        out_shape=jax.ShapeDtypeStruct((nblk * bm, D_MODEL), F32),
        compiler_params=_cparams(("parallel",)), name="experts")(block_e, xs, wg, wu, wd)


def _combine_body(d_cur, d_next, h_ref, route_ref, ys_hbm, y_ref, ybuf, sem, *, tm):
    j = pl.program_id(0)
    nblk = pl.num_programs(0)
    slot = j % 2

    def gather(idx_ref, s):
        def start_row(g, r):
            pltpu.make_async_copy(ys_hbm.at[pl.ds(idx_ref[0, g * SUBLANES + r], 1)], ybuf.at[s, g, pl.ds(r, 1)],
                                  sem.at[s]).start()
        _for_rows(2 * tm, start_row)

    @pl.when(j == 0)
    def _():
        gather(d_cur, 0)

    @pl.when(j + 1 < nblk)
    def _():
        gather(d_next, 1 - slot)

    _wait_rows(ys_hbm, 2 * tm, sem.at[slot])
    route = route_ref[...]
    w1 = route[:, 2:3]
    w2 = route[:, 3:4]
    ng = tm // SUBLANES
    y1 = ybuf[slot, 0:ng].reshape(tm, D_MODEL)
    y2 = ybuf[slot, ng:2 * ng].reshape(tm, D_MODEL)
    y_ref[...] = h_ref[...] + (y1 * w1 + y2 * w2)


def _tile_slots(dest, tm):
    nblk = dest.shape[1] // tm
    return dest.reshape(2, nblk, tm).transpose(1, 0, 2).reshape(nblk, 1, 2 * tm)


def _combine(h, route, ys, dest, tm):
    n = h.shape[0]
    nblk = n // tm
    d3 = _tile_slots(dest, tm)
    return pl.pallas_call(
        functools.partial(_combine_body, tm=tm), grid=(nblk,),
        in_specs=[
            pl.BlockSpec((None, 1, 2 * tm), lambda j: (j, 0, 0), memory_space=pltpu.SMEM),
            pl.BlockSpec((None, 1, 2 * tm), lambda j: (jnp.minimum(j + 1, nblk - 1), 0, 0), memory_space=pltpu.SMEM),
            pl.BlockSpec((tm, D_MODEL), lambda j: (j, 0)),
            pl.BlockSpec((tm, LANES), lambda j: (j, 0)),
            pl.BlockSpec(memory_space=pl.ANY),
        ],
        out_specs=pl.BlockSpec((tm, D_MODEL), lambda j: (j, 0)),
        out_shape=jax.ShapeDtypeStruct((n, D_MODEL), F32),
        scratch_shapes=[pltpu.VMEM((2, 2 * tm // SUBLANES, SUBLANES, D_MODEL), F32), pltpu.SemaphoreType.DMA((2,))],
        compiler_params=_cparams(("arbitrary",)), name="combine")(d3, d3, h, route, ys)


def _dispatch_plan(route, bm):
    n = route.shape[0]
    flat_e = jnp.concatenate([route[:, 0], route[:, 1]]).astype(jnp.int32)
    n_asg = flat_e.shape[0]
    onehot = (flat_e[:, None] == jnp.arange(N_EXPERTS, dtype=jnp.int32)[None, :]).astype(jnp.int32)
    csum = jnp.cumsum(onehot, axis=0)
    rank = jnp.sum(csum * onehot, axis=1) - 1
    counts = csum[-1]
    padded = (counts + bm - 1) // bm * bm
    pad_end = jnp.cumsum(padded)
    dest = jnp.sum((pad_end - padded)[None, :] * onehot, axis=1) + rank
    nblk = -(-n_asg // bm) + N_EXPERTS
    first_slot = jnp.arange(nblk, dtype=jnp.int32) * bm
    block_e = jnp.minimum(jnp.sum((pad_end[None, :] <= first_slot[:, None]).astype(jnp.int32), axis=1),
                          N_EXPERTS - 1)
    pad_cum = jnp.cumsum(padded - counts)
    i = jnp.arange(nblk * bm - n_asg, dtype=jnp.int32)
    seg = jnp.sum((pad_cum[None, :] <= i[:, None]).astype(jnp.int32), axis=1)
    seg_first_slot = jnp.concatenate([pad_end - padded + counts, pad_end[-1:]])
    seg_first_i = jnp.concatenate([jnp.zeros((1,), jnp.int32), pad_cum])
    unused = seg_first_slot[seg] + i - seg_first_i[seg]
    return dest.reshape(2, n), block_e, unused


def _hmoe(groups, wts, bm, tm):
    route_all = jnp.concatenate([route for _, _, route in groups], axis=0)
    dest_all, block_e, unused = _dispatch_plan(route_all, bm)
    xs = _dispatch([xt for _, xt, _ in groups], _tile_slots(dest_all, tm), unused, block_e.shape[0] * bm, tm)
    ys = _experts(xs, block_e, wts['wg'], wts['wu'], wts['wd'], bm)
    ys_out, off = [], 0
    for h, _, route in groups:
        n = h.shape[0]
        ys_out.append(_combine(h, route, ys, dest_all[:, off:off + n], tm))
        off += n
    return ys_out


def _prep_weights(norm_mix, w_in, qn_a, kn_a, qn_c, kn_c, gla_gate_up, gla_gate_bias, gla_norm, mem_norm,
                  w_mem_kv, w_branch_a, w_branch_b, w_branch_c, w_out, norm_ffn, w_router_group, b_router_group,
                  w_router_expert, b_router_expert, w_exp_gate, w_exp_up, w_exp_down):
    splits = (A_WIDTH, A_WIDTH, A_WIDTH, B_KW, B_KW, B_VW, B_VW, B_RANK, C_WIDTH, 3 * D_MODEL)
    offs = np.concatenate([[0], np.cumsum(splits)])
    names = ('wqa', 'wka', 'wva', 'wqb', 'wkb', 'wvb', 'wrb', 'wab', 'wqc', 'wgl')
    wts = {nm: w_in[:, int(offs[i]):int(offs[i + 1])].astype(BF16) for i, nm in enumerate(names)}
    wts['wab'] = jnp.pad(wts['wab'], ((0, 0), (0, LANES - B_RANK)))
    wts['gup'] = jnp.pad(gla_gate_up.astype(F32), ((0, LANES - B_RANK), (0, 0)))
    wts['gbias'] = gla_gate_bias.reshape(1, B_KW)
    wts['nm'] = norm_mix.reshape(1, D_MODEL)
    wts['qna'] = jnp.tile(qn_a, 2).reshape(1, LANES)
    wts['kna'] = jnp.tile(kn_a, 2).reshape(1, LANES)
    wts['qnc'] = qn_c.reshape(1, C_HEAD_DIM)
    wts['knc'] = kn_c.reshape(1, C_HEAD_DIM)
    wts['gn'] = gla_norm.reshape(1, B_DV)
    wts['mn'] = mem_norm.reshape(1, D_MODEL)
    wts['wmem'] = w_mem_kv.astype(BF16)
    wts['wba'] = w_branch_a.astype(BF16)
    wts['wbb'] = w_branch_b.astype(BF16)
    wts['wbc'] = w_branch_c.astype(BF16)
    wts['wout'] = w_out.astype(BF16)
    wts['nf'] = norm_ffn.reshape(1, D_MODEL)
    wr = jnp.concatenate([w_router_expert, w_router_group], axis=1)
    br = jnp.concatenate([b_router_expert, b_router_group])
    npad = LANES - N_EXPERTS - N_GROUPS_E
    wr = jnp.pad(wr, ((0, 0), (0, npad)))
    wr_hi = wr.astype(BF16)
    wts['wr'] = jnp.stack([wr_hi, (wr - wr_hi.astype(F32)).astype(BF16)])
    wts['br'] = jnp.pad(br, (0, npad)).reshape(1, LANES)
    wts['wg'] = w_exp_gate.astype(BF16)
    wts['wu'] = w_exp_up.astype(BF16)
    wts['wd'] = w_exp_down.astype(BF16)
    return wts


def _tile(n, want):
    t = min(n, want)
    assert n % t == 0
    return t


def _layer(x, wts, attend, gla_state, mem_kv, cfg):
    bsz, seq, _ = x.shape
    n = bsz * seq
    x2d = x.reshape(n, D_MODEL)
    qa, ka, va, qb, kb, vb, rb, la, qc = _in_proj(x2d, wts, _tile(n, cfg['tm_in']))

    def r3(a):
        return a.reshape(bsz, seq, a.shape[-1])

    def r4(a):
        return a.reshape(a.shape[0], bsz, seq, LANES)

    oa, extras = attend(r4(qa), r4(ka), r4(va))
    ob, s_fin = _gla(r3(qb), r3(kb), r3(vb), r3(la), r3(rb), wts['gn'], gla_state, cfg['gla_bb'],
                     _tile(seq, cfg['gla_tc']))
    oc = _cross(r3(qc), mem_kv, cfg['cross_bb'], _tile(seq, cfg['cross_tq']))
    pre_ffn = _merge(x2d, oa.reshape(n, A_GW), ob.reshape(n, B_VW), oc.reshape(n, C_WIDTH), wts,
                     _tile(n, cfg['tm_merge']))
    return pre_ffn, extras, s_fin


def kernel(x_prompt, x_sample, mem_prompt, cache_win1, cache_win2, cache_win3, state_gla, cache_mem, norm_mix, w_in, qn_a, kn_a, qn_c, kn_c, gla_gate_up, gla_gate_bias, gla_norm, mem_norm, w_mem_kv, w_branch_a, w_branch_b, w_branch_c, w_out, norm_ffn, w_router_group, b_router_group, w_router_expert, b_router_expert, w_exp_gate, w_exp_up, w_exp_down):
    wts = _prep_weights(norm_mix, w_in, qn_a, kn_a, qn_c, kn_c, gla_gate_up, gla_gate_bias, gla_norm, mem_norm,
                        w_mem_kv, w_branch_a, w_branch_b, w_branch_c, w_out, norm_ffn, w_router_group,
                        b_router_group, w_router_expert, b_router_expert, w_exp_gate, w_exp_up, w_exp_down)
    bsz, seq, _ = x_prompt.shape
    bd, n_new, _ = x_sample.shape

    mem_p = _mem_kv(mem_prompt.reshape(bsz * N_MEM, D_MODEL), wts['mn'], wts['wmem'], wts['knc'], 256)
    mem_p = mem_p.reshape(bsz, N_MEM * 2 * C_HEADS, C_HEAD_DIM)

    def attend_prompt(qa, ka, va):
        oa = _dilated_prompt(qa, ka, va)
        bufs = []
        for gi, (w, _) in enumerate(A_GROUPS):
            wb = min(w, seq)
            kv = jnp.concatenate([a[2 * gi + c, :, seq - wb:] for a in (ka, va) for c in range(2)], axis=-1)
            bufs.append(kv.reshape(bsz, wb, 2, A_HEADS, A_HEAD_DIM))
        return oa, bufs

    cfg_p = dict(tm_in=256, tm_merge=512, gla_bb=2, gla_tc=256, cross_bb=1, cross_tq=512)
    s0 = jnp.zeros((bsz, B_HEADS, B_DK, B_DV), F32)
    ffn_in_p, wins_p, gla_p = _layer(x_prompt, wts, attend_prompt, s0, mem_p, cfg_p)

    def attend_sample(qa, ka, va):
        return _dilated_sample(qa, ka, va, (cache_win1, cache_win2, cache_win3), bb=1)

    cfg_s = dict(tm_in=256, tm_merge=256, gla_bb=8, gla_tc=n_new, cross_bb=8, cross_tq=n_new)
    ffn_in_s, wins_s, gla_s = _layer(x_sample, wts, attend_sample, state_gla,
                                     cache_mem.reshape(bd, N_MEM * 2 * C_HEADS, C_HEAD_DIM), cfg_s)

    y_p, y_s = _hmoe([ffn_in_p, ffn_in_s], wts, bm=256, tm=256)
    y_p = y_p.reshape(x_prompt.shape)
    y_s = y_s.reshape(x_sample.shape)

    return (y_p, y_s, wins_p[0], wins_p[1], wins_p[2], gla_p, mem_p.reshape(bsz, N_MEM, 2, C_HEADS, C_HEAD_DIM),
            wins_s[0], wins_s[1], wins_s[2], gla_s)
```

```python
import functools

import jax
import jax.numpy as jnp
import numpy as np
from jax import lax
from jax.experimental import pallas as pl
from jax.experimental.pallas import tpu as pltpu

F32 = jnp.float32
BF16 = jnp.bfloat16
HIGHEST = lax.Precision.HIGHEST

D_MODEL = 1024
A_GROUPS = ((128, 1), (512, 4), (2048, 16))
A_J = 128
A_HEADS = 4
A_HEAD_DIM = 64
A_GW = A_HEADS * A_HEAD_DIM
A_WIDTH = 3 * A_GW
B_HEADS = 4
B_DK = 64
B_DV = 128
B_KW = B_HEADS * B_DK
B_VW = B_HEADS * B_DV
B_RANK = 16
B_TAU = 16.0
B_CHUNK = 64
C_HEADS = 4
C_HEAD_DIM = 128
C_WIDTH = C_HEADS * C_HEAD_DIM
N_MEM = 256
N_GROUPS_E = 4
E_PER_GROUP = 8
N_EXPERTS = 32
D_EXPERT = 256
EPS = 1e-6
NEG = -1e30
LOG2E = 1.4426950408889634
LANES = 128
VMEM_LIMIT = 56 * 1024 * 1024

NT = (((1,), (1,)), ((), ()))
TN = (((0,), (0,)), ((), ()))


def _cparams(sem):
    return pltpu.CompilerParams(dimension_semantics=sem, vmem_limit_bytes=VMEM_LIMIT)


def _const_spec(shape, single=True):
    nd = len(shape)
    if single:
        return pl.BlockSpec(shape, lambda *_: (0,) * nd, pipeline_mode=pl.Buffered(1))
    return pl.BlockSpec(shape, lambda *_: (0,) * nd)


def _rms(x):
    return x * lax.rsqrt(jnp.mean(x * x, axis=-1, keepdims=True) + EPS)


def _headnorm64(h, g128, scale, o_ref):
    lo = lax.broadcasted_iota(jnp.int32, (1, LANES), 1) < A_HEAD_DIM
    for c in range(h.shape[1] // LANES):
        x = h[:, c * LANES:(c + 1) * LANES]
        x2 = x * x
        s_lo = jnp.sum(jnp.where(lo, x2, 0.0), axis=-1, keepdims=True)
        s_hi = jnp.sum(jnp.where(lo, 0.0, x2), axis=-1, keepdims=True)
        ms = jnp.where(lo, s_lo, s_hi) * (1.0 / A_HEAD_DIM)
        y = x * lax.rsqrt(ms + EPS) * g128
        if scale != 1.0:
            y = y * scale
        o_ref[c] = y.astype(o_ref.dtype)


def _in_proj_body(x_ref, nm_ref, wqa, wka, wva, wqb, wkb, wvb, wrb, wab, wqc, qna, kna, qnc, gup, gbias,
                  qa_o, ka_o, va_o, qb_o, kb_o, vb_o, rb_o, la_o, qc_o):
    xn = (_rms(x_ref[...]) * nm_ref[...]).astype(BF16)

    def mm(w):
        return jnp.dot(xn, w[...], preferred_element_type=F32)

    _headnorm64(mm(wqa), qna[...], A_HEAD_DIM ** -0.5, qa_o)
    _headnorm64(mm(wka), kna[...], 1.0, ka_o)
    hv = mm(wva)
    for c in range(A_WIDTH // LANES):
        va_o[c] = hv[:, c * LANES:(c + 1) * LANES]
    qb_o[...] = mm(wqb) * (B_DK ** -0.5)
    kb_o[...] = mm(wkb)
    vb_o[...] = mm(wvb)
    rb_o[...] = mm(wrb)
    z = jnp.dot(mm(wab), gup[...], precision=HIGHEST, preferred_element_type=F32) + gbias[...]
    la_o[...] = (jnp.minimum(z, 0.0) - jnp.log1p(jnp.exp(-jnp.abs(z)))) * (1.0 / B_TAU)
    hq = mm(wqc)
    for h in range(C_HEADS):
        sl = slice(h * C_HEAD_DIM, (h + 1) * C_HEAD_DIM)
        qc_o[:, sl] = _rms(hq[:, sl]) * qnc[...] * (C_HEAD_DIM ** -0.5)


def _in_proj(x2d, wts, tm):
    n = x2d.shape[0]
    ins = [x2d, wts['nm'], wts['wqa'], wts['wka'], wts['wva'], wts['wqb'], wts['wkb'], wts['wvb'], wts['wrb'],
           wts['wab'], wts['wqc'], wts['qna'], wts['kna'], wts['qnc'], wts['gup'], wts['gbias']]
    in_specs = [pl.BlockSpec((tm, D_MODEL), lambda i: (i, 0))] + [_const_spec(a.shape, single=False) for a in ins[1:]]
    nch = A_WIDTH // LANES
    widths = [B_KW, B_KW, B_VW, B_VW, B_KW, C_WIDTH]
    out_shape = ([jax.ShapeDtypeStruct((nch, n, LANES), F32)] * 3
                 + [jax.ShapeDtypeStruct((n, w), F32) for w in widths])
    out_specs = ([pl.BlockSpec((nch, tm, LANES), lambda i: (0, i, 0))] * 3
                 + [pl.BlockSpec((tm, w), lambda i: (i, 0)) for w in widths])
    return pl.pallas_call(
        _in_proj_body, grid=(n // tm,), in_specs=in_specs, out_specs=out_specs, out_shape=out_shape,
        compiler_params=_cparams(("parallel",)), name="in_proj")(*ins)


def _dil_prompt_body(q_ref, k_ref, v_ref, o_ref, w1_ref, w2_ref, w3_ref, os_ref, ms_ref, ls_ref, *, seq):
    g = pl.program_id(1)
    win_refs = (w1_ref, w2_ref, w3_ref)
    J = A_J
    row = lax.broadcasted_iota(jnp.int32, (J, J), 0)
    col = lax.broadcasted_iota(jnp.int32, (J, J), 1)
    cur_ok = col <= row
    lo = lax.broadcasted_iota(jnp.int32, (1, LANES), 1) < A_HEAD_DIM
    hi = jnp.logical_not(lo)

    for gi, (_, r) in enumerate(A_GROUPS):
        nb = seq // (r * J)

        @pl.when(g == gi)
        def _(gi=gi, r=r, nb=nb):
            def blk(i, carry):
                rho = i >> (nb.bit_length() - 1)
                n = i & (nb - 1)
                start = rho + n * (r * J)
                pstart = jnp.maximum(start - r * J, 0)
                if r == 1:
                    start, pstart = pl.multiple_of(start, J), pl.multiple_of(pstart, J)
                prev_ok = col >= row + jnp.where(n > 0, 0, J)

                def ld(ref, c, s):
                    return ref[c, 0, pl.ds(s, J, stride=r), :]

                heads = []
                for c in range(2):
                    q = ld(q_ref, c, start) * LOG2E
                    kc, vc = ld(k_ref, c, start).astype(BF16), ld(v_ref, c, start).astype(BF16)
                    kp, vp = ld(k_ref, c, pstart).astype(BF16), ld(v_ref, c, pstart).astype(BF16)
                    for lanes_ok in (lo, hi):
                        qh = jnp.where(lanes_ok, q, 0.0).astype(BF16)
                        s_c = lax.dot_general(qh, kc, NT, preferred_element_type=F32)
                        s_p = lax.dot_general(qh, kp, NT, preferred_element_type=F32)
                        heads.append((s_c, s_p, vc, vp))
                outs = []
                for s_c, s_p, vc, vp in heads:
                    s_c = jnp.where(cur_ok, s_c, NEG)
                    s_p = jnp.where(prev_ok, s_p, NEG)
                    m = jnp.max(jnp.maximum(s_c, s_p), axis=-1, keepdims=True)
                    p_c = jnp.exp2(s_c - m)
                    p_p = jnp.exp2(s_p - m)
                    l = jnp.sum(p_c + p_p, axis=-1, keepdims=True)
                    pv = (jnp.dot(p_c.astype(BF16), vc, preferred_element_type=F32)
                          + jnp.dot(p_p.astype(BF16), vp, preferred_element_type=F32))
                    outs.append((pv, m, l))
                for c in range(2):
                    (o0, m0, l0), (o1, m1, l1) = outs[2 * c], outs[2 * c + 1]
                    os_ref[gi, c, pl.ds(start, J, stride=r), :] = jnp.where(lo, o0, o1)
                    ms_ref[gi, c, pl.ds(start, J, stride=r), :] = jnp.where(lo, m0, m1)
                    ls_ref[gi, c, pl.ds(start, J, stride=r), :] = jnp.where(lo, l0, l1)
                return carry

            lax.fori_loop(0, seq // J, blk, 0, unroll=4)

            wb = win_refs[gi].shape[-1]
            for c in range(2):
                rows = slice(c * LANES, (c + 1) * LANES)
                win_refs[gi][0, rows, :] = k_ref[c, 0, seq - wb:seq, :].T
                win_refs[gi][0, A_GW + c * LANES:A_GW + (c + 1) * LANES, :] = v_ref[c, 0, seq - wb:seq, :].T

    @pl.when(g == len(A_GROUPS) - 1)
    def _():
        def comb(i, carry):
            sl = pl.ds(pl.multiple_of(i * J, J), J)
            for c in range(2):
                m0, m1, m2 = ms_ref[0, c, sl, :], ms_ref[1, c, sl, :], ms_ref[2, c, sl, :]
                mx = jnp.maximum(jnp.maximum(m0, m1), m2)
                w0, w1, w2 = jnp.exp2(m0 - mx), jnp.exp2(m1 - mx), jnp.exp2(m2 - mx)
                num = w0 * os_ref[0, c, sl, :] + w1 * os_ref[1, c, sl, :] + w2 * os_ref[2, c, sl, :]
                den = w0 * ls_ref[0, c, sl, :] + w1 * ls_ref[1, c, sl, :] + w2 * ls_ref[2, c, sl, :]
                o_ref[0, sl, c * LANES:(c + 1) * LANES] = (num / den).astype(o_ref.dtype)
            return carry

        lax.fori_loop(0, seq // J, comb, 0)


def _dilated_prompt(qa, ka, va):
    _, bsz, seq, _ = qa.shape
    ng = len(A_GROUPS)
    spec = pl.BlockSpec((2, 1, seq, LANES), lambda b, g: (g, b, 0, 0))
    wbs = [min(w, seq) for w, _ in A_GROUPS]
    oa, *wins = pl.pallas_call(
        functools.partial(_dil_prompt_body, seq=seq), grid=(bsz, ng),
        in_specs=[spec, spec, spec],
        out_specs=[pl.BlockSpec((1, seq, A_GW), lambda b, g: (b, 0, 0))]
        + [pl.BlockSpec((1, 2 * A_GW, wb), lambda b, g: (b, 0, 0)) for wb in wbs],
        out_shape=[jax.ShapeDtypeStruct((bsz, seq, A_GW), BF16)]
        + [jax.ShapeDtypeStruct((bsz, 2 * A_GW, wb), F32) for wb in wbs],
        scratch_shapes=[pltpu.VMEM((ng, 2, seq, LANES), F32)] * 3,
        compiler_params=_cparams(("parallel", "arbitrary")), name="dilated_prompt")(qa, ka, va)
    return oa, [w.reshape(bsz, 2, A_HEADS, A_HEAD_DIM, w.shape[-1]).transpose(0, 4, 1, 2, 3) for w in wins]


def _dil_sample_body(q_ref, k_ref, v_ref, c1_ref, c2_ref, c3_ref, o_ref, n1_ref, n2_ref, n3_ref, *, bb, n_new):
    L = n_new
    caches = (c1_ref, c2_ref, c3_ref)
    outs = (n1_ref, n2_ref, n3_ref)
    nq = A_HEADS * L
    head_of_lane = lax.broadcasted_iota(jnp.int32, (1, A_GW), 1) >> (A_HEAD_DIM.bit_length() - 1)
    q_of_row = lax.broadcasted_iota(jnp.int32, (nq, 1), 0) & (L - 1)
    t_new = lax.broadcasted_iota(jnp.int32, (nq, L), 1)
    lane = lax.broadcasted_iota(jnp.int32, (1, LANES), 1)
    pad_rows = jnp.zeros((LANES - L, 2 * A_GW), F32)

    def per_head_rows(x):
        return jnp.concatenate([jnp.where(head_of_lane == h, x, 0.0) for h in range(A_HEADS)], axis=0)

    for b in range(bb):
        scores, scores_n, vts, vns = [], [], [], []
        for gi, (w, r) in enumerate(A_GROUPS):
            cref = caches[gi]
            qg, kn, vn = [jnp.concatenate([ref[2 * gi, b], ref[2 * gi + 1, b]], axis=-1)
                          for ref in (q_ref, k_ref, v_ref)]
            qbd = per_head_rows(qg).astype(BF16)
            kt = cref[b, 0:A_GW, :].astype(BF16)
            s = jnp.dot(qbd, kt, preferred_element_type=F32)
            s_n = lax.dot_general(qbd, kn.astype(BF16), NT, preferred_element_type=F32)
            pos = lax.broadcasted_iota(jnp.int32, (nq, w), 1)
            s = jnp.where((pos >= q_of_row) & (((q_of_row - pos) & (r - 1)) == 0), s, NEG)
            s_n = jnp.where((t_new <= q_of_row) & (((q_of_row - t_new) & (r - 1)) == 0), s_n, NEG)
            scores.append(s)
            scores_n.append(s_n)
            vts.append(cref[b, A_GW:2 * A_GW, :].astype(BF16))
            vns.append(vn.astype(BF16))

            rolled = pltpu.roll(cref[b], w - L, axis=1)
            outs[gi][b] = rolled
            new_t = jnp.concatenate([jnp.concatenate([kn, vn], axis=-1), pad_rows], axis=0).T
            outs[gi][b, :, w - LANES:w] = jnp.where(lane < LANES - L, rolled[:, w - LANES:w],
                                                    pltpu.roll(new_t, LANES - L, axis=1))

        m = None
        for s in scores + scores_n:
            ms = jnp.max(s, axis=-1, keepdims=True)
            m = ms if m is None else jnp.maximum(m, ms)
        den = 0.0
        acc = jnp.zeros((nq, A_GW), F32)
        for s, s_n, vt, vn in zip(scores, scores_n, vts, vns):
            p = jnp.exp(s - m)
            p_n = jnp.exp(s_n - m)
            den = den + jnp.sum(p, axis=-1, keepdims=True) + jnp.sum(p_n, axis=-1, keepdims=True)
            acc = acc + lax.dot_general(p.astype(BF16), vt, NT, preferred_element_type=F32)
            acc = acc + jnp.dot(p_n.astype(BF16), vn, preferred_element_type=F32)
        acc = acc / den
        o = jnp.zeros((L, A_GW), F32)
        for h in range(A_HEADS):
            o = o + jnp.where(head_of_lane == h, acc[h * L:(h + 1) * L, :], 0.0)
        o_ref[b] = o.astype(o_ref.dtype)


def _dilated_sample(qa, ka, va, caches, bb):
    _, bd, n_new, _ = qa.shape
    cts = [c.transpose(0, 2, 3, 4, 1).reshape(bd, 2 * A_GW, c.shape[1]) for c in caches]
    new_spec = pl.BlockSpec((A_WIDTH // LANES, bb, n_new, LANES), lambda i: (0, i, 0, 0))
    cspecs = [pl.BlockSpec((bb,) + c.shape[1:], lambda i: (i, 0, 0)) for c in cts]
    oa, *nts = pl.pallas_call(
        functools.partial(_dil_sample_body, bb=bb, n_new=n_new), grid=(bd // bb,),
        in_specs=[new_spec, new_spec, new_spec] + cspecs,
        out_specs=[pl.BlockSpec((bb, n_new, A_GW), lambda i: (i, 0, 0))] + cspecs,
        out_shape=[jax.ShapeDtypeStruct((bd, n_new, A_GW), F32)] + [jax.ShapeDtypeStruct(c.shape, F32) for c in cts],
        compiler_params=_cparams(("parallel",)), name="dilated_sample")(qa, ka, va, *cts)
    news = [n.reshape(bd, 2, A_HEADS, A_HEAD_DIM, n.shape[-1]).transpose(0, 4, 1, 2, 3) for n in nts]
    return oa, news


def _gla_body(q_ref, k_ref, v_ref, la_ref, rb_ref, gn_ref, s0_ref, o_ref, sf_ref, s_ref, *, bb, tc, chunk):
    t = pl.program_id(1)
    C = chunk

    @pl.when(t == 0)
    def _():
        s_ref[...] = s0_ref[...]

    causal = lax.broadcasted_iota(jnp.int32, (C, C), 0) >= lax.broadcasted_iota(jnp.int32, (C, C), 1)
    lo = lax.broadcasted_iota(jnp.int32, (1, LANES), 1) < B_DK
    head_lanes = (lo, jnp.logical_not(lo))
    row_s = lax.broadcasted_iota(jnp.int32, (LANES, LANES), 0)
    eye = row_s == lax.broadcasted_iota(jnp.int32, (LANES, LANES), 1)
    top = row_s < B_DK
    pos = lax.broadcasted_iota(jnp.int32, (tc, 1), 0) & (C - 1)
    mid = C // 2 - 1
    gn = gn_ref[...]
    nc = tc // C

    def per_chunk_row(x, r):
        return jnp.concatenate([jnp.broadcast_to(x[c * C + r:c * C + r + 1, :], (C, x.shape[1])) for c in range(nc)],
                               axis=0)

    for b in range(bb):
        cum = la_ref[b]
        step = 1
        while step < C:
            cum = cum + jnp.where(pos >= step, pltpu.roll(cum, step, axis=0), 0.0)
            step *= 2
        cmid = per_chunk_row(cum, mid)
        clast = per_chunk_row(cum, C - 1)
        q = q_ref[b]
        k = k_ref[b]
        q_in = q * jnp.exp(cum)
        q_t = q * jnp.exp(cum - cmid)
        k_t = (k * jnp.exp(cmid - cum)).astype(BF16)
        k_l = (k * jnp.exp(clast - cum)).astype(BF16)
        e_last = jnp.exp(clast)

        o_intra = {}
        upd = {}
        for c in range(nc):
            rows = slice(c * C, (c + 1) * C)
            for p in range(B_HEADS // 2):
                pl_ = slice(p * LANES, (p + 1) * LANES)
                kl = k_l[rows, pl_]
                tn = []
                for hh in range(2):
                    h = 2 * p + hh
                    v = v_ref[b, rows, h * B_DV:(h + 1) * B_DV].astype(BF16)
                    qt = jnp.where(head_lanes[hh], q_t[rows, pl_], 0.0).astype(BF16)
                    att = lax.dot_general(qt, k_t[rows, pl_], NT, preferred_element_type=F32)
                    att = jnp.where(causal, att, 0.0).astype(BF16)
                    o_intra[c, h] = jnp.dot(att, v, preferred_element_type=F32)
                    tn.append(lax.dot_general(kl, v, TN, preferred_element_type=F32))
                upd[c, p] = jnp.where(top, tn[0], tn[1])

        for c in range(nc):
            rows = slice(c * C, (c + 1) * C)
            for p in range(B_HEADS // 2):
                pl_ = slice(p * LANES, (p + 1) * LANES)
                S = s_ref[b, p]
                S16 = S.astype(BF16)
                e_col = jnp.sum(jnp.where(eye, e_last[c * C:c * C + 1, pl_], 0.0), axis=-1, keepdims=True)
                s_ref[b, p] = e_col * S + upd[c, p]
                for hh in range(2):
                    h = 2 * p + hh
                    vs = slice(h * B_DV, (h + 1) * B_DV)
                    qi = jnp.where(head_lanes[hh], q_in[rows, pl_], 0.0).astype(BF16)
                    o = o_intra[c, h] + jnp.dot(qi, S16, preferred_element_type=F32)
                    mu = jnp.mean(o, axis=-1, keepdims=True)
                    d = o - mu
                    var = jnp.mean(d * d, axis=-1, keepdims=True)
                    rb = rb_ref[b, rows, vs]
                    o_ref[b, rows, vs] = (d * lax.rsqrt(var + EPS) * gn * (rb * jax.nn.sigmoid(rb))).astype(o_ref.dtype)

    @pl.when(t == pl.num_programs(1) - 1)
    def _():
        sf_ref[...] = s_ref[...]


def _gla(qb, kb, vb, la, rb, gn, s0, bb, tc):
    bsz, seq, _ = qb.shape
    chunk = int(np.gcd(seq, B_CHUNK))
    kspec = pl.BlockSpec((bb, tc, B_KW), lambda i, t: (i, t, 0))
    vspec = pl.BlockSpec((bb, tc, B_VW), lambda i, t: (i, t, 0))
    pair_shape = (B_HEADS // 2, 2 * B_DK, B_DV)
    sspec = pl.BlockSpec((bb,) + pair_shape, lambda i, t: (i, 0, 0, 0))
    o, s_fin = pl.pallas_call(
        functools.partial(_gla_body, bb=bb, tc=tc, chunk=chunk), grid=(bsz // bb, seq // tc),
        in_specs=[kspec, kspec, vspec, kspec, vspec, _const_spec(gn.shape), sspec],
        out_specs=[vspec, sspec],
        out_shape=[jax.ShapeDtypeStruct((bsz, seq, B_VW), BF16), jax.ShapeDtypeStruct((bsz,) + pair_shape, F32)],
        scratch_shapes=[pltpu.VMEM((bb,) + pair_shape, F32)],
        compiler_params=_cparams(("parallel", "arbitrary")), name="gla")(
            qb, kb, vb, la, rb, gn, s0.reshape((bsz,) + pair_shape))
    return o, s_fin.reshape(s0.shape)


def _mem_kv_body(m_ref, mn_ref, w_ref, knc_ref, o_ref):
    xn = (_rms(m_ref[...]) * mn_ref[...]).astype(BF16)
    kv = jnp.dot(xn, w_ref[...], preferred_element_type=F32)
    tm = kv.shape[0]
    nrow = 2 * C_HEADS
    for j in range(nrow):
        x = kv[:, j * C_HEAD_DIM:(j + 1) * C_HEAD_DIM]
        if j < C_HEADS:
            x = _rms(x) * knc_ref[...]
        o_ref[pl.ds(j, tm, stride=nrow), :] = x


def _mem_kv(mem2d, mn, w, knc, tm):
    n = mem2d.shape[0]
    nrow = 2 * C_HEADS
    return pl.pallas_call(
        _mem_kv_body, grid=(n // tm,),
        in_specs=[pl.BlockSpec((tm, D_MODEL), lambda i: (i, 0)), _const_spec(mn.shape), _const_spec(w.shape),
                  _const_spec(knc.shape)],
        out_specs=pl.BlockSpec((tm * nrow, C_HEAD_DIM), lambda i: (i, 0)),
        out_shape=jax.ShapeDtypeStruct((n * nrow, C_HEAD_DIM), F32),
        compiler_params=_cparams(("parallel",)), name="mem_kv")(mem2d, mn, w, knc)


def _cross_body(q_ref, mem_ref, o_ref, *, bb):
    for b in range(bb):
        q = q_ref[b].astype(BF16)
        for h in range(C_HEADS):
            sl = slice(h * C_HEAD_DIM, (h + 1) * C_HEAD_DIM)
            kh = mem_ref[b, pl.ds(h, N_MEM, stride=2 * C_HEADS), :].astype(BF16)
            vh = mem_ref[b, pl.ds(C_HEADS + h, N_MEM, stride=2 * C_HEADS), :].astype(BF16)
            s = lax.dot_general(q[:, sl], kh, NT, preferred_element_type=F32)
            p = jnp.exp(s - jnp.max(s, axis=-1, keepdims=True))
            l = jnp.sum(p, axis=-1, keepdims=True)
            o = jnp.dot(p.astype(BF16), vh, preferred_element_type=F32) / l
            o_ref[b, :, sl] = o.astype(o_ref.dtype)


def _cross(qc, mem_kv, bb, tq):
    bsz, seq, _ = qc.shape
    return pl.pallas_call(
        functools.partial(_cross_body, bb=bb), grid=(bsz // bb, seq // tq),
        in_specs=[pl.BlockSpec((bb, tq, C_WIDTH), lambda i, t: (i, t, 0)),
                  pl.BlockSpec((bb,) + mem_kv.shape[1:], lambda i, t: (i, 0, 0))],
        out_specs=pl.BlockSpec((bb, tq, C_WIDTH), lambda i, t: (i, t, 0)),
        out_shape=jax.ShapeDtypeStruct((bsz, seq, C_WIDTH), F32),
        compiler_params=_cparams(("parallel", "arbitrary")), name="cross")(qc, mem_kv)


def _merge_body(x_ref, nm_ref, wgl, oa_ref, ob_ref, oc_ref, wba, wbb, wbc, wout, nf_ref, wr, br,
                h_o, xt_o, route_o):
    x = x_ref[...]
    xn = (_rms(x) * nm_ref[...]).astype(BF16)
    merged = None
    for i, (o_ref, w_ref) in enumerate(((oa_ref, wba), (ob_ref, wbb), (oc_ref, wbc))):
        gate = jax.nn.sigmoid(jnp.dot(xn, wgl[:, i * D_MODEL:(i + 1) * D_MODEL], preferred_element_type=F32))
        term = gate * jnp.dot(o_ref[...].astype(BF16), w_ref[...], preferred_element_type=F32)
        merged = term if merged is None else merged + term
    h = x + jnp.dot(merged.astype(BF16), wout[...], preferred_element_type=F32)
    h_o[...] = h
    xt = _rms(h) * nf_ref[...]
    xt_o[...] = xt
    xt_hi = xt.astype(BF16)
    xt_lo = (xt - xt_hi.astype(F32)).astype(BF16)
    logits = (jnp.dot(xt_hi, wr[0], preferred_element_type=F32) + jnp.dot(xt_hi, wr[1], preferred_element_type=F32)
              + jnp.dot(xt_lo, wr[0], preferred_element_type=F32) + br[...])
    lane = lax.broadcasted_iota(jnp.int32, logits.shape, 1)
    lane_f = lane.astype(F32)
    big = float(LANES)
    is_g = (lane >= N_EXPERTS) & (lane < N_EXPERTS + N_GROUPS_E)
    glog = jnp.where(is_g, logits, NEG)
    gmax = jnp.max(glog, axis=-1, keepdims=True)
    grp = jnp.min(jnp.where(glog == gmax, lane_f, big), axis=-1, keepdims=True) - float(N_EXPERTS)
    g_w = 1.0 / jnp.sum(jnp.where(is_g, jnp.exp(glog - gmax), 0.0), axis=-1, keepdims=True)
    in_grp = (lane < N_EXPERTS) & (jnp.floor(lane_f * (1.0 / E_PER_GROUP)) == grp)
    elog = jnp.where(in_grp, logits, NEG)
    v1 = jnp.max(elog, axis=-1, keepdims=True)
    i1 = jnp.min(jnp.where(elog == v1, lane_f, big), axis=-1, keepdims=True)
    elog2 = jnp.where(lane_f == i1, NEG, elog)
    v2 = jnp.max(elog2, axis=-1, keepdims=True)
    i2 = jnp.min(jnp.where(elog2 == v2, lane_f, big), axis=-1, keepdims=True)
    e2 = jnp.exp(v2 - v1)
    w1 = g_w / (1.0 + e2)
    w2 = g_w * e2 / (1.0 + e2)
    route_o[...] = jnp.where(lane == 0, i1, jnp.where(lane == 1, i2, jnp.where(lane == 2, w1,
                             jnp.where(lane == 3, w2, 0.0))))


def _merge(x2d, oa, ob, oc, wts, tm):
    n = x2d.shape[0]
    consts = [wts['nm'], wts['wgl']]
    consts2 = [wts['wba'], wts['wbb'], wts['wbc'], wts['wout'], wts['nf'], wts['wr'], wts['br']]

    def tile(w):
        return pl.BlockSpec((tm, w), lambda i: (i, 0))

    in_specs = ([tile(D_MODEL)] + [_const_spec(a.shape) for a in consts]
                + [tile(A_GW), tile(B_VW), tile(C_WIDTH)] + [_const_spec(a.shape) for a in consts2])
    return pl.pallas_call(
        _merge_body, grid=(n // tm,), in_specs=in_specs,
        out_specs=[tile(D_MODEL), tile(D_MODEL), tile(LANES)],
        out_shape=[jax.ShapeDtypeStruct((n, D_MODEL), F32), jax.ShapeDtypeStruct((n, D_MODEL), F32),
                   jax.ShapeDtypeStruct((n, LANES), F32)],
        compiler_params=_cparams(("parallel",)), name="merge")(x2d, *consts, oa, ob, oc, *consts2)


SUBLANES = 8


def _for_rows(n_rows, fn):
    def body(g, carry):
        for r in range(SUBLANES):
            fn(g, r)
        return carry
    lax.fori_loop(0, n_rows // SUBLANES, body, 0)


def _wait_rows(hbm, rows, sem):
    pltpu.make_async_copy(hbm.at[pl.ds(0, rows)], hbm.at[pl.ds(0, rows)], sem).wait()


def _dispatch_body(*refs, tm, tiles, n_zero, zero_steps):
    dest_ref, zslot_ref = refs[0], refs[1]
    xt_refs = refs[2:2 + len(tiles)]
    xs_out, stage, zrow, sem = refs[2 + len(tiles):]
    j = pl.program_id(0)
    nsteps = pl.num_programs(0)
    slot = j % 2

    def wait_step(step, s):
        _wait_rows(xs_out, 2 * tm, sem.at[s])

        @pl.when(step < zero_steps)
        def _():
            _wait_rows(xs_out, n_zero, sem.at[s])

    @pl.when(j >= 2)
    def _():
        wait_step(j - 2, slot)

    @pl.when(j == 0)
    def _():
        zrow[...] = jnp.zeros_like(zrow)

    first = 0
    for xt_ref, nt in zip(xt_refs, tiles):
        @pl.when(jnp.logical_and(j >= first, j < first + nt))
        def _(xt_ref=xt_ref):
            stage[slot] = xt_ref[...].reshape(stage.shape[1:])
        first += nt

    for k in range(2):
        def start_row(g, r, k=k):
            pltpu.make_async_copy(stage.at[slot, g, pl.ds(r, 1)],
                                  xs_out.at[pl.ds(dest_ref[0, k * tm + g * SUBLANES + r], 1)], sem.at[slot]).start()
        _for_rows(tm, start_row)

    @pl.when(j < zero_steps)
    def _():
        def start_zero(g, r):
            pltpu.make_async_copy(zrow.at[pl.ds(0, 1)], xs_out.at[pl.ds(zslot_ref[0, g * SUBLANES + r], 1)],
                                  sem.at[slot]).start()
        _for_rows(n_zero, start_zero)

    @pl.when(j == nsteps - 1)
    def _():
        wait_step(j, slot)

        @pl.when(j >= 1)
        def _():
            wait_step(j - 1, 1 - slot)


def _dispatch(xts, dest3, unused, n_slots, tm):
    tiles = [xt.shape[0] // tm for xt in xts]
    nsteps = sum(tiles)
    zero_steps = tiles[0]
    assert unused.shape[0] % (zero_steps * SUBLANES) == 0
    n_zero = unused.shape[0] // zero_steps
    in_specs = [pl.BlockSpec((None, 1, 2 * tm), lambda j: (j, 0, 0), memory_space=pltpu.SMEM),
                pl.BlockSpec((None, 1, n_zero), lambda j: (jnp.minimum(j, zero_steps - 1), 0, 0),
                             memory_space=pltpu.SMEM)]
    first = 0
    for nt in tiles:
        in_specs.append(pl.BlockSpec((tm, D_MODEL), lambda j, first=first, nt=nt: (jnp.clip(j - first, 0, nt - 1), 0)))
        first += nt
    return pl.pallas_call(
        functools.partial(_dispatch_body, tm=tm, tiles=tiles, n_zero=n_zero, zero_steps=zero_steps), grid=(nsteps,),
        in_specs=in_specs,
        out_specs=pl.BlockSpec(memory_space=pl.ANY),
        out_shape=jax.ShapeDtypeStruct((n_slots, D_MODEL), F32),
        scratch_shapes=[pltpu.VMEM((2, tm // SUBLANES, SUBLANES, D_MODEL), F32),
                        pltpu.VMEM((SUBLANES, D_MODEL), F32), pltpu.SemaphoreType.DMA((2,))],
        compiler_params=_cparams(("arbitrary",)), name="dispatch")(
            dest3, unused.reshape(zero_steps, 1, n_zero), *xts)


def _expert_body(be_ref, xs_ref, wg_ref, wu_ref, wd_ref, y_ref):
    del be_ref
    xb = xs_ref[...].astype(BF16)
    gate = jnp.dot(xb, wg_ref[0], preferred_element_type=F32)
    hid = gate * jax.nn.sigmoid(gate) * jnp.dot(xb, wu_ref[0], preferred_element_type=F32)
    y_ref[...] = jnp.dot(hid.astype(BF16), wd_ref[0], preferred_element_type=F32)


def _experts(xs, block_e, wg, wu, wd, bm):
    nblk = block_e.shape[0]
    up_spec = pl.BlockSpec((1, D_MODEL, D_EXPERT), lambda j, be: (be[j], 0, 0))
    grid_spec = pltpu.PrefetchScalarGridSpec(
        num_scalar_prefetch=1, grid=(nblk,),
        in_specs=[pl.BlockSpec((bm, D_MODEL), lambda j, be: (j, 0)), up_spec, up_spec,
                  pl.BlockSpec((1, D_EXPERT, D_MODEL), lambda j, be: (be[j], 0, 0))],
        out_specs=pl.BlockSpec((bm, D_MODEL), lambda j, be: (j, 0)))
    return pl.pallas_call(
        _expert_body, grid_spec=grid_spec,
        out_shape=jax.ShapeDtypeStruct((nblk * bm, D_MODEL), F32),
        compiler_params=_cparams(("parallel",)), name="experts")(block_e, xs, wg, wu, wd)


def _combine_body(d_cur, d_next, h_ref, route_ref, ys_hbm, y_ref, ybuf, sem, *, tm):
    j = pl.program_id(0)
    nblk = pl.num_programs(0)
    slot = j % 2

    def gather(idx_ref, s):
        def start_row(g, r):
            pltpu.make_async_copy(ys_hbm.at[pl.ds(idx_ref[0, g * SUBLANES + r], 1)], ybuf.at[s, g, pl.ds(r, 1)],
                                  sem.at[s]).start()
        _for_rows(2 * tm, start_row)

    @pl.when(j == 0)
    def _():
        gather(d_cur, 0)

    @pl.when(j + 1 < nblk)
    def _():
        gather(d_next, 1 - slot)

    _wait_rows(ys_hbm, 2 * tm, sem.at[slot])
    route = route_ref[...]
    w1 = route[:, 2:3]
    w2 = route[:, 3:4]
    ng = tm // SUBLANES
    y1 = ybuf[slot, 0:ng].reshape(tm, D_MODEL)
    y2 = ybuf[slot, ng:2 * ng].reshape(tm, D_MODEL)
    y_ref[...] = h_ref[...] + (y1 * w1 + y2 * w2)


def _tile_slots(dest, tm):
    nblk = dest.shape[1] // tm
    return dest.reshape(2, nblk, tm).transpose(1, 0, 2).reshape(nblk, 1, 2 * tm)


def _combine(h, route, ys, dest, tm):
    n = h.shape[0]
    nblk = n // tm
    d3 = _tile_slots(dest, tm)
    return pl.pallas_call(
        functools.partial(_combine_body, tm=tm), grid=(nblk,),
        in_specs=[
            pl.BlockSpec((None, 1, 2 * tm), lambda j: (j, 0, 0), memory_space=pltpu.SMEM),
            pl.BlockSpec((None, 1, 2 * tm), lambda j: (jnp.minimum(j + 1, nblk - 1), 0, 0), memory_space=pltpu.SMEM),
            pl.BlockSpec((tm, D_MODEL), lambda j: (j, 0)),
            pl.BlockSpec((tm, LANES), lambda j: (j, 0)),
            pl.BlockSpec(memory_space=pl.ANY),
        ],
        out_specs=pl.BlockSpec((tm, D_MODEL), lambda j: (j, 0)),
        out_shape=jax.ShapeDtypeStruct((n, D_MODEL), F32),
        scratch_shapes=[pltpu.VMEM((2, 2 * tm // SUBLANES, SUBLANES, D_MODEL), F32), pltpu.SemaphoreType.DMA((2,))],
        compiler_params=_cparams(("arbitrary",)), name="combine")(d3, d3, h, route, ys)


def _dispatch_plan(route, bm):
    n = route.shape[0]
    flat_e = jnp.concatenate([route[:, 0], route[:, 1]]).astype(jnp.int32)
    n_asg = flat_e.shape[0]
    onehot = (flat_e[:, None] == jnp.arange(N_EXPERTS, dtype=jnp.int32)[None, :]).astype(jnp.int32)
    csum = jnp.cumsum(onehot, axis=0)
    rank = jnp.sum(csum * onehot, axis=1) - 1
    counts = csum[-1]
    padded = (counts + bm - 1) // bm * bm
    pad_end = jnp.cumsum(padded)
    dest = jnp.sum((pad_end - padded)[None, :] * onehot, axis=1) + rank
    nblk = -(-n_asg // bm) + N_EXPERTS
    first_slot = jnp.arange(nblk, dtype=jnp.int32) * bm
    block_e = jnp.minimum(jnp.sum((pad_end[None, :] <= first_slot[:, None]).astype(jnp.int32), axis=1),
                          N_EXPERTS - 1)
    pad_cum = jnp.cumsum(padded - counts)
    i = jnp.arange(nblk * bm - n_asg, dtype=jnp.int32)
    seg = jnp.sum((pad_cum[None, :] <= i[:, None]).astype(jnp.int32), axis=1)
    seg_first_slot = jnp.concatenate([pad_end - padded + counts, pad_end[-1:]])
    seg_first_i = jnp.concatenate([jnp.zeros((1,), jnp.int32), pad_cum])
    unused = seg_first_slot[seg] + i - seg_first_i[seg]
    return dest.reshape(2, n), block_e, unused


def _hmoe(groups, wts, bm, tm):
    route_all = jnp.concatenate([route for _, _, route in groups], axis=0)
    dest_all, block_e, unused = _dispatch_plan(route_all, bm)
    xs = _dispatch([xt for _, xt, _ in groups], _tile_slots(dest_all, tm), unused, block_e.shape[0] * bm, tm)
    ys = _experts(xs, block_e, wts['wg'], wts['wu'], wts['wd'], bm)
    ys_out, off = [], 0
    for h, _, route in groups:
        n = h.shape[0]
        ys_out.append(_combine(h, route, ys, dest_all[:, off:off + n], tm))
        off += n
    return ys_out


def _prep_weights(norm_mix, w_in, qn_a, kn_a, qn_c, kn_c, gla_gate_up, gla_gate_bias, gla_norm, mem_norm,
                  w_mem_kv, w_branch_a, w_branch_b, w_branch_c, w_out, norm_ffn, w_router_group, b_router_group,
                  w_router_expert, b_router_expert, w_exp_gate, w_exp_up, w_exp_down):
    splits = (A_WIDTH, A_WIDTH, A_WIDTH, B_KW, B_KW, B_VW, B_VW, B_RANK, C_WIDTH, 3 * D_MODEL)
    offs = np.concatenate([[0], np.cumsum(splits)])
    names = ('wqa', 'wka', 'wva', 'wqb', 'wkb', 'wvb', 'wrb', 'wab', 'wqc', 'wgl')
    wts = {nm: w_in[:, int(offs[i]):int(offs[i + 1])].astype(BF16) for i, nm in enumerate(names)}
    wts['wab'] = jnp.pad(wts['wab'], ((0, 0), (0, LANES - B_RANK)))
    wts['gup'] = jnp.pad(gla_gate_up.astype(F32), ((0, LANES - B_RANK), (0, 0)))
    wts['gbias'] = gla_gate_bias.reshape(1, B_KW)
    wts['nm'] = norm_mix.reshape(1, D_MODEL)
    wts['qna'] = jnp.tile(qn_a, 2).reshape(1, LANES)
    wts['kna'] = jnp.tile(kn_a, 2).reshape(1, LANES)
    wts['qnc'] = qn_c.reshape(1, C_HEAD_DIM)
    wts['knc'] = kn_c.reshape(1, C_HEAD_DIM)
    wts['gn'] = gla_norm.reshape(1, B_DV)
    wts['mn'] = mem_norm.reshape(1, D_MODEL)
    wts['wmem'] = w_mem_kv.astype(BF16)
    wts['wba'] = w_branch_a.astype(BF16)
    wts['wbb'] = w_branch_b.astype(BF16)
    wts['wbc'] = w_branch_c.astype(BF16)
    wts['wout'] = w_out.astype(BF16)
    wts['nf'] = norm_ffn.reshape(1, D_MODEL)
    wr = jnp.concatenate([w_router_expert, w_router_group], axis=1)
    br = jnp.concatenate([b_router_expert, b_router_group])
    npad = LANES - N_EXPERTS - N_GROUPS_E
    wr = jnp.pad(wr, ((0, 0), (0, npad)))
    wr_hi = wr.astype(BF16)
    wts['wr'] = jnp.stack([wr_hi, (wr - wr_hi.astype(F32)).astype(BF16)])
    wts['br'] = jnp.pad(br, (0, npad)).reshape(1, LANES)
    wts['wg'] = w_exp_gate.astype(BF16)
    wts['wu'] = w_exp_up.astype(BF16)
    wts['wd'] = w_exp_down.astype(BF16)
    return wts


def _tile(n, want):
    t = min(n, want)
    assert n % t == 0
    return t


def _layer(x, wts, attend, gla_state, mem_kv, cfg):
    bsz, seq, _ = x.shape
    n = bsz * seq
    x2d = x.reshape(n, D_MODEL)
    qa, ka, va, qb, kb, vb, rb, la, qc = _in_proj(x2d, wts, _tile(n, cfg['tm_in']))

    def r3(a):
        return a.reshape(bsz, seq, a.shape[-1])

    def r4(a):
        return a.reshape(a.shape[0], bsz, seq, LANES)

    oa, extras = attend(r4(qa), r4(ka), r4(va))
    ob, s_fin = _gla(r3(qb), r3(kb), r3(vb), r3(la), r3(rb), wts['gn'], gla_state, cfg['gla_bb'],
                     _tile(seq, cfg['gla_tc']))
    oc = _cross(r3(qc), mem_kv, cfg['cross_bb'], _tile(seq, cfg['cross_tq']))
    pre_ffn = _merge(x2d, oa.reshape(n, A_GW), ob.reshape(n, B_VW), oc.reshape(n, C_WIDTH), wts,
                     _tile(n, cfg['tm_merge']))
    return pre_ffn, extras, s_fin


def kernel(x_prompt, x_sample, mem_prompt, cache_win1, cache_win2, cache_win3, state_gla, cache_mem, norm_mix, w_in, qn_a, kn_a, qn_c, kn_c, gla_gate_up, gla_gate_bias, gla_norm, mem_norm, w_mem_kv, w_branch_a, w_branch_b, w_branch_c, w_out, norm_ffn, w_router_group, b_router_group, w_router_expert, b_router_expert, w_exp_gate, w_exp_up, w_exp_down):
    wts = _prep_weights(norm_mix, w_in, qn_a, kn_a, qn_c, kn_c, gla_gate_up, gla_gate_bias, gla_norm, mem_norm,
                        w_mem_kv, w_branch_a, w_branch_b, w_branch_c, w_out, norm_ffn, w_router_group,
                        b_router_group, w_router_expert, b_router_expert, w_exp_gate, w_exp_up, w_exp_down)
    bsz, seq, _ = x_prompt.shape
    bd, n_new, _ = x_sample.shape

    mem_p = _mem_kv(mem_prompt.reshape(bsz * N_MEM, D_MODEL), wts['mn'], wts['wmem'], wts['knc'], 256)
    mem_p = mem_p.reshape(bsz, N_MEM * 2 * C_HEADS, C_HEAD_DIM)

    cfg_p = dict(tm_in=256, tm_merge=512, gla_bb=2, gla_tc=256, cross_bb=1, cross_tq=512)
    s0 = jnp.zeros((bsz, B_HEADS, B_DK, B_DV), F32)
    ffn_in_p, wins_p, gla_p = _layer(x_prompt, wts, _dilated_prompt, s0, mem_p, cfg_p)

    def attend_sample(qa, ka, va):
        return _dilated_sample(qa, ka, va, (cache_win1, cache_win2, cache_win3), bb=1)

    cfg_s = dict(tm_in=256, tm_merge=256, gla_bb=8, gla_tc=n_new, cross_bb=8, cross_tq=n_new)
    ffn_in_s, wins_s, gla_s = _layer(x_sample, wts, attend_sample, state_gla,
                                     cache_mem.reshape(bd, N_MEM * 2 * C_HEADS, C_HEAD_DIM), cfg_s)

    y_p, y_s = _hmoe([ffn_in_p, ffn_in_s], wts, bm=256, tm=256)
    y_p = y_p.reshape(x_prompt.shape)
    y_s = y_s.reshape(x_sample.shape)

    return (y_p, y_s, wins_p[0], wins_p[1], wins_p[2], gla_p, mem_p.reshape(bsz, N_MEM, 2, C_HEADS, C_HEAD_DIM),
            wins_s[0], wins_s[1], wins_s[2], gla_s)
```

```python
import functools

import jax
import jax.numpy as jnp
import numpy as np
from jax import lax
from jax.experimental import pallas as pl
from jax.experimental.pallas import tpu as pltpu

F32 = jnp.float32
BF16 = jnp.bfloat16
HIGHEST = lax.Precision.HIGHEST

D_MODEL = 1024
A_GROUPS = ((128, 1), (512, 4), (2048, 16))
A_J = 128
A_HEADS = 4
A_HEAD_DIM = 64
A_GW = A_HEADS * A_HEAD_DIM
A_WIDTH = 3 * A_GW
B_HEADS = 4
B_DK = 64
B_DV = 128
B_KW = B_HEADS * B_DK
B_VW = B_HEADS * B_DV
B_RANK = 16
B_TAU = 16.0
B_CHUNK = 64
C_HEADS = 4
C_HEAD_DIM = 128
C_WIDTH = C_HEADS * C_HEAD_DIM
N_MEM = 256
N_GROUPS_E = 4
E_PER_GROUP = 8
N_EXPERTS = 32
D_EXPERT = 256
EPS = 1e-6
NEG = -1e30
LOG2E = 1.4426950408889634
LANES = 128
VMEM_LIMIT = 56 * 1024 * 1024

NT = (((1,), (1,)), ((), ()))
TN = (((0,), (0,)), ((), ()))


def _cparams(sem):
    return pltpu.CompilerParams(dimension_semantics=sem, vmem_limit_bytes=VMEM_LIMIT)


def _const_spec(shape, single=True):
    nd = len(shape)
    if single:
        return pl.BlockSpec(shape, lambda *_: (0,) * nd, pipeline_mode=pl.Buffered(1))
    return pl.BlockSpec(shape, lambda *_: (0,) * nd)


def _rms(x):
    return x * lax.rsqrt(jnp.mean(x * x, axis=-1, keepdims=True) + EPS)


def _headnorm64(h, g128, scale, o_ref):
    lo = lax.broadcasted_iota(jnp.int32, (1, LANES), 1) < A_HEAD_DIM
    for c in range(h.shape[1] // LANES):
        x = h[:, c * LANES:(c + 1) * LANES]
        x2 = x * x
        s_lo = jnp.sum(jnp.where(lo, x2, 0.0), axis=-1, keepdims=True)
        s_hi = jnp.sum(jnp.where(lo, 0.0, x2), axis=-1, keepdims=True)
        ms = jnp.where(lo, s_lo, s_hi) * (1.0 / A_HEAD_DIM)
        y = x * lax.rsqrt(ms + EPS) * g128
        if scale != 1.0:
            y = y * scale
        o_ref[c] = y.astype(o_ref.dtype)


def _in_proj_body(x_ref, nm_ref, wqa, wka, wva, wqb, wkb, wvb, wrb, wab, wqc, qna, kna, qnc, gup, gbias,
                  qa_o, ka_o, va_o, qb_o, kb_o, vb_o, rb_o, la_o, qc_o):
    xn = (_rms(x_ref[...]) * nm_ref[...]).astype(BF16)

    def mm(w):
        return jnp.dot(xn, w[...], preferred_element_type=F32)

    _headnorm64(mm(wqa), qna[...], A_HEAD_DIM ** -0.5, qa_o)
    _headnorm64(mm(wka), kna[...], 1.0, ka_o)
    hv = mm(wva)
    for c in range(A_WIDTH // LANES):
        va_o[c] = hv[:, c * LANES:(c + 1) * LANES]
    qb_o[...] = mm(wqb) * (B_DK ** -0.5)
    kb_o[...] = mm(wkb)
    vb_o[...] = mm(wvb)
    rb_o[...] = mm(wrb)
    z = jnp.dot(mm(wab), gup[...], precision=HIGHEST, preferred_element_type=F32) + gbias[...]
    la_o[...] = (jnp.minimum(z, 0.0) - jnp.log1p(jnp.exp(-jnp.abs(z)))) * (1.0 / B_TAU)
    hq = mm(wqc)
    for h in range(C_HEADS):
        sl = slice(h * C_HEAD_DIM, (h + 1) * C_HEAD_DIM)
        qc_o[:, sl] = _rms(hq[:, sl]) * qnc[...] * (C_HEAD_DIM ** -0.5)


def _in_proj(x2d, wts, tm):
    n = x2d.shape[0]
    ins = [x2d, wts['nm'], wts['wqa'], wts['wka'], wts['wva'], wts['wqb'], wts['wkb'], wts['wvb'], wts['wrb'],
           wts['wab'], wts['wqc'], wts['qna'], wts['kna'], wts['qnc'], wts['gup'], wts['gbias']]
    in_specs = [pl.BlockSpec((tm, D_MODEL), lambda i: (i, 0))] + [_const_spec(a.shape, single=False) for a in ins[1:]]
    nch = A_WIDTH // LANES
    widths = [B_KW, B_KW, B_VW, B_VW, B_KW, C_WIDTH]
    out_shape = ([jax.ShapeDtypeStruct((nch, n, LANES), F32)] * 3
                 + [jax.ShapeDtypeStruct((n, w), F32) for w in widths])
    out_specs = ([pl.BlockSpec((nch, tm, LANES), lambda i: (0, i, 0))] * 3
                 + [pl.BlockSpec((tm, w), lambda i: (i, 0)) for w in widths])
    return pl.pallas_call(
        _in_proj_body, grid=(n // tm,), in_specs=in_specs, out_specs=out_specs, out_shape=out_shape,
        compiler_params=_cparams(("parallel",)), name="in_proj")(*ins)


def _dil_prompt_body(q_ref, k_ref, v_ref, o_ref, w1_ref, w2_ref, w3_ref, os_ref, ms_ref, ls_ref, *, seq):
    g = pl.program_id(1)
    win_refs = (w1_ref, w2_ref, w3_ref)
    J = A_J
    row = lax.broadcasted_iota(jnp.int32, (J, J), 0)
    col = lax.broadcasted_iota(jnp.int32, (J, J), 1)
    cur_ok = col <= row
    lo = lax.broadcasted_iota(jnp.int32, (1, LANES), 1) < A_HEAD_DIM
    hi = jnp.logical_not(lo)

    for gi, (_, r) in enumerate(A_GROUPS):
        nb = seq // (r * J)

        @pl.when(g == gi)
        def _(gi=gi, r=r, nb=nb):
            def blk(i, carry):
                rho = i >> (nb.bit_length() - 1)
                n = i & (nb - 1)
                start = rho + n * (r * J)
                pstart = jnp.maximum(start - r * J, 0)
                if r == 1:
                    start, pstart = pl.multiple_of(start, J), pl.multiple_of(pstart, J)
                prev_ok = col >= row + jnp.where(n > 0, 0, J)

                def ld(ref, c, s):
                    return ref[c, 0, pl.ds(s, J, stride=r), :]

                heads = []
                for c in range(2):
                    q = ld(q_ref, c, start) * LOG2E
                    kc, vc = ld(k_ref, c, start).astype(BF16), ld(v_ref, c, start).astype(BF16)
                    kp, vp = ld(k_ref, c, pstart).astype(BF16), ld(v_ref, c, pstart).astype(BF16)
                    for lanes_ok in (lo, hi):
                        qh = jnp.where(lanes_ok, q, 0.0).astype(BF16)
                        s_c = lax.dot_general(qh, kc, NT, preferred_element_type=F32)
                        s_p = lax.dot_general(qh, kp, NT, preferred_element_type=F32)
                        heads.append((s_c, s_p, vc, vp))
                outs = []
                for s_c, s_p, vc, vp in heads:
                    s_c = jnp.where(cur_ok, s_c, NEG)
                    s_p = jnp.where(prev_ok, s_p, NEG)
                    m = jnp.max(jnp.maximum(s_c, s_p), axis=-1, keepdims=True)
                    p_c = jnp.exp2(s_c - m)
                    p_p = jnp.exp2(s_p - m)
                    l = jnp.sum(p_c + p_p, axis=-1, keepdims=True)
                    pv = (jnp.dot(p_c.astype(BF16), vc, preferred_element_type=F32)
                          + jnp.dot(p_p.astype(BF16), vp, preferred_element_type=F32))
                    outs.append((pv, m, l))
                for c in range(2):
                    (o0, m0, l0), (o1, m1, l1) = outs[2 * c], outs[2 * c + 1]
                    os_ref[gi, c, pl.ds(start, J, stride=r), :] = jnp.where(lo, o0, o1)
                    ms_ref[gi, c, pl.ds(start, J, stride=r), :] = jnp.where(lo, m0, m1)
                    ls_ref[gi, c, pl.ds(start, J, stride=r), :] = jnp.where(lo, l0, l1)
                return carry

            lax.fori_loop(0, seq // J, blk, 0, unroll=4)

            wb = win_refs[gi].shape[-1]
            for c in range(2):
                rows = slice(c * LANES, (c + 1) * LANES)
                win_refs[gi][0, rows, :] = k_ref[c, 0, seq - wb:seq, :].T
                win_refs[gi][0, A_GW + c * LANES:A_GW + (c + 1) * LANES, :] = v_ref[c, 0, seq - wb:seq, :].T

    @pl.when(g == len(A_GROUPS) - 1)
    def _():
        def comb(i, carry):
            sl = pl.ds(pl.multiple_of(i * J, J), J)
            for c in range(2):
                m0, m1, m2 = ms_ref[0, c, sl, :], ms_ref[1, c, sl, :], ms_ref[2, c, sl, :]
                mx = jnp.maximum(jnp.maximum(m0, m1), m2)
                w0, w1, w2 = jnp.exp2(m0 - mx), jnp.exp2(m1 - mx), jnp.exp2(m2 - mx)
                num = w0 * os_ref[0, c, sl, :] + w1 * os_ref[1, c, sl, :] + w2 * os_ref[2, c, sl, :]
                den = w0 * ls_ref[0, c, sl, :] + w1 * ls_ref[1, c, sl, :] + w2 * ls_ref[2, c, sl, :]
                o_ref[0, sl, c * LANES:(c + 1) * LANES] = (num / den).astype(o_ref.dtype)
            return carry

        lax.fori_loop(0, seq // J, comb, 0)


def _dilated_prompt(qa, ka, va):
    _, bsz, seq, _ = qa.shape
    ng = len(A_GROUPS)
    spec = pl.BlockSpec((2, 1, seq, LANES), lambda b, g: (g, b, 0, 0))
    wbs = [min(w, seq) for w, _ in A_GROUPS]
    oa, *wins = pl.pallas_call(
        functools.partial(_dil_prompt_body, seq=seq), grid=(bsz, ng),
        in_specs=[spec, spec, spec],
        out_specs=[pl.BlockSpec((1, seq, A_GW), lambda b, g: (b, 0, 0))]
        + [pl.BlockSpec((1, 2 * A_GW, wb), lambda b, g: (b, 0, 0)) for wb in wbs],
        out_shape=[jax.ShapeDtypeStruct((bsz, seq, A_GW), BF16)]
        + [jax.ShapeDtypeStruct((bsz, 2 * A_GW, wb), F32) for wb in wbs],
        scratch_shapes=[pltpu.VMEM((ng, 2, seq, LANES), F32)] * 3,
        compiler_params=_cparams(("parallel", "arbitrary")), name="dilated_prompt")(qa, ka, va)
    return oa, [w.reshape(bsz, 2, A_HEADS, A_HEAD_DIM, w.shape[-1]).transpose(0, 4, 1, 2, 3) for w in wins]


def _dil_sample_body(q_ref, k_ref, v_ref, c1_ref, c2_ref, c3_ref, o_ref, n1_ref, n2_ref, n3_ref, *, bb, n_new):
    L = n_new
    caches = (c1_ref, c2_ref, c3_ref)
    outs = (n1_ref, n2_ref, n3_ref)
    nq = A_HEADS * L
    head_of_lane = lax.broadcasted_iota(jnp.int32, (1, A_GW), 1) >> (A_HEAD_DIM.bit_length() - 1)
    q_of_row = lax.broadcasted_iota(jnp.int32, (nq, 1), 0) & (L - 1)
    t_new = lax.broadcasted_iota(jnp.int32, (nq, L), 1)
    lane = lax.broadcasted_iota(jnp.int32, (1, LANES), 1)
    pad_rows = jnp.zeros((LANES - L, 2 * A_GW), F32)

    def per_head_rows(x):
        return jnp.concatenate([jnp.where(head_of_lane == h, x, 0.0) for h in range(A_HEADS)], axis=0)

    for b in range(bb):
        scores, scores_n, vts, vns = [], [], [], []
        for gi, (w, r) in enumerate(A_GROUPS):
            cref = caches[gi]
            qg, kn, vn = [jnp.concatenate([ref[2 * gi, b], ref[2 * gi + 1, b]], axis=-1)
                          for ref in (q_ref, k_ref, v_ref)]
            qbd = per_head_rows(qg).astype(BF16)
            kt = cref[b, 0:A_GW, :].astype(BF16)
            s = jnp.dot(qbd, kt, preferred_element_type=F32)
            s_n = lax.dot_general(qbd, kn.astype(BF16), NT, preferred_element_type=F32)
            pos = lax.broadcasted_iota(jnp.int32, (nq, w), 1)
            s = jnp.where((pos >= q_of_row) & (((q_of_row - pos) & (r - 1)) == 0), s, NEG)
            s_n = jnp.where((t_new <= q_of_row) & (((q_of_row - t_new) & (r - 1)) == 0), s_n, NEG)
            scores.append(s)
            scores_n.append(s_n)
            vts.append(cref[b, A_GW:2 * A_GW, :].astype(BF16))
            vns.append(vn.astype(BF16))

            rolled = pltpu.roll(cref[b], w - L, axis=1)
            outs[gi][b] = rolled
            new_t = jnp.concatenate([jnp.concatenate([kn, vn], axis=-1), pad_rows], axis=0).T
            outs[gi][b, :, w - LANES:w] = jnp.where(lane < LANES - L, rolled[:, w - LANES:w],
                                                    pltpu.roll(new_t, LANES - L, axis=1))

        m = None
        for s in scores + scores_n:
            ms = jnp.max(s, axis=-1, keepdims=True)
            m = ms if m is None else jnp.maximum(m, ms)
        den = 0.0
        acc = jnp.zeros((nq, A_GW), F32)
        for s, s_n, vt, vn in zip(scores, scores_n, vts, vns):
            p = jnp.exp(s - m)
            p_n = jnp.exp(s_n - m)
            den = den + jnp.sum(p, axis=-1, keepdims=True) + jnp.sum(p_n, axis=-1, keepdims=True)
            acc = acc + lax.dot_general(p.astype(BF16), vt, NT, preferred_element_type=F32)
            acc = acc + jnp.dot(p_n.astype(BF16), vn, preferred_element_type=F32)
        acc = acc / den
        o = jnp.zeros((L, A_GW), F32)
        for h in range(A_HEADS):
            o = o + jnp.where(head_of_lane == h, acc[h * L:(h + 1) * L, :], 0.0)
        o_ref[b] = o.astype(o_ref.dtype)


def _dilated_sample(qa, ka, va, caches, bb):
    _, bd, n_new, _ = qa.shape
    cts = [c.transpose(0, 2, 3, 4, 1).reshape(bd, 2 * A_GW, c.shape[1]) for c in caches]
    new_spec = pl.BlockSpec((A_WIDTH // LANES, bb, n_new, LANES), lambda i: (0, i, 0, 0))
    cspecs = [pl.BlockSpec((bb,) + c.shape[1:], lambda i: (i, 0, 0)) for c in cts]
    oa, *nts = pl.pallas_call(
        functools.partial(_dil_sample_body, bb=bb, n_new=n_new), grid=(bd // bb,),
        in_specs=[new_spec, new_spec, new_spec] + cspecs,
        out_specs=[pl.BlockSpec((bb, n_new, A_GW), lambda i: (i, 0, 0))] + cspecs,
        out_shape=[jax.ShapeDtypeStruct((bd, n_new, A_GW), F32)] + [jax.ShapeDtypeStruct(c.shape, F32) for c in cts],
        compiler_params=_cparams(("parallel",)), name="dilated_sample")(qa, ka, va, *cts)
    news = [n.reshape(bd, 2, A_HEADS, A_HEAD_DIM, n.shape[-1]).transpose(0, 4, 1, 2, 3) for n in nts]
    return oa, news


def _gla_body(q_ref, k_ref, v_ref, la_ref, rb_ref, gn_ref, s0_ref, o_ref, sf_ref, s_ref, *, bb, tc, chunk):
    t = pl.program_id(1)
    C = chunk

    @pl.when(t == 0)
    def _():
        s_ref[...] = s0_ref[...]

    causal = lax.broadcasted_iota(jnp.int32, (C, C), 0) >= lax.broadcasted_iota(jnp.int32, (C, C), 1)
    lo = lax.broadcasted_iota(jnp.int32, (1, LANES), 1) < B_DK
    head_lanes = (lo, jnp.logical_not(lo))
    row_s = lax.broadcasted_iota(jnp.int32, (LANES, LANES), 0)
    eye = row_s == lax.broadcasted_iota(jnp.int32, (LANES, LANES), 1)
    top = row_s < B_DK
    pos = lax.broadcasted_iota(jnp.int32, (tc, 1), 0) & (C - 1)
    mid = C // 2 - 1
    gn = gn_ref[...]
    nc = tc // C

    def per_chunk_row(x, r):
        return jnp.concatenate([jnp.broadcast_to(x[c * C + r:c * C + r + 1, :], (C, x.shape[1])) for c in range(nc)],
                               axis=0)

    for b in range(bb):
        cum = la_ref[b]
        step = 1
        while step < C:
            cum = cum + jnp.where(pos >= step, pltpu.roll(cum, step, axis=0), 0.0)
            step *= 2
        cmid = per_chunk_row(cum, mid)
        clast = per_chunk_row(cum, C - 1)
        q = q_ref[b]
        k = k_ref[b]
        q_in = q * jnp.exp(cum)
        q_t = q * jnp.exp(cum - cmid)
        k_t = (k * jnp.exp(cmid - cum)).astype(BF16)
        k_l = (k * jnp.exp(clast - cum)).astype(BF16)
        e_last = jnp.exp(clast)

        o_intra = {}
        upd = {}
        for c in range(nc):
            rows = slice(c * C, (c + 1) * C)
            for p in range(B_HEADS // 2):
                pl_ = slice(p * LANES, (p + 1) * LANES)
                kl = k_l[rows, pl_]
                tn = []
                for hh in range(2):
                    h = 2 * p + hh
                    v = v_ref[b, rows, h * B_DV:(h + 1) * B_DV].astype(BF16)
                    qt = jnp.where(head_lanes[hh], q_t[rows, pl_], 0.0).astype(BF16)
                    att = lax.dot_general(qt, k_t[rows, pl_], NT, preferred_element_type=F32)
                    att = jnp.where(causal, att, 0.0).astype(BF16)
                    o_intra[c, h] = jnp.dot(att, v, preferred_element_type=F32)
                    tn.append(lax.dot_general(kl, v, TN, preferred_element_type=F32))
                upd[c, p] = jnp.where(top, tn[0], tn[1])

        for c in range(nc):
            rows = slice(c * C, (c + 1) * C)
            for p in range(B_HEADS // 2):
                pl_ = slice(p * LANES, (p + 1) * LANES)
                S = s_ref[b, p]
                S16 = S.astype(BF16)
                e_col = jnp.sum(jnp.where(eye, e_last[c * C:c * C + 1, pl_], 0.0), axis=-1, keepdims=True)
                s_ref[b, p] = e_col * S + upd[c, p]
                for hh in range(2):
                    h = 2 * p + hh
                    vs = slice(h * B_DV, (h + 1) * B_DV)
                    qi = jnp.where(head_lanes[hh], q_in[rows, pl_], 0.0).astype(BF16)
                    o = o_intra[c, h] + jnp.dot(qi, S16, preferred_element_type=F32)
                    mu = jnp.mean(o, axis=-1, keepdims=True)
                    d = o - mu
                    var = jnp.mean(d * d, axis=-1, keepdims=True)
                    rb = rb_ref[b, rows, vs]
                    o_ref[b, rows, vs] = (d * lax.rsqrt(var + EPS) * gn * (rb * jax.nn.sigmoid(rb))).astype(o_ref.dtype)

    @pl.when(t == pl.num_programs(1) - 1)
    def _():
        sf_ref[...] = s_ref[...]


def _gla(qb, kb, vb, la, rb, gn, s0, bb, tc):
    bsz, seq, _ = qb.shape
    chunk = int(np.gcd(seq, B_CHUNK))
    kspec = pl.BlockSpec((bb, tc, B_KW), lambda i, t: (i, t, 0))
    vspec = pl.BlockSpec((bb, tc, B_VW), lambda i, t: (i, t, 0))
    pair_shape = (B_HEADS // 2, 2 * B_DK, B_DV)
    sspec = pl.BlockSpec((bb,) + pair_shape, lambda i, t: (i, 0, 0, 0))
    o, s_fin = pl.pallas_call(
        functools.partial(_gla_body, bb=bb, tc=tc, chunk=chunk), grid=(bsz // bb, seq // tc),
        in_specs=[kspec, kspec, vspec, kspec, vspec, _const_spec(gn.shape), sspec],
        out_specs=[vspec, sspec],
        out_shape=[jax.ShapeDtypeStruct((bsz, seq, B_VW), BF16), jax.ShapeDtypeStruct((bsz,) + pair_shape, F32)],
        scratch_shapes=[pltpu.VMEM((bb,) + pair_shape, F32)],
        compiler_params=_cparams(("parallel", "arbitrary")), name="gla")(
            qb, kb, vb, la, rb, gn, s0.reshape((bsz,) + pair_shape))
    return o, s_fin.reshape(s0.shape)


def _mem_kv_body(m_ref, mn_ref, w_ref, knc_ref, o_ref):
    xn = (_rms(m_ref[...]) * mn_ref[...]).astype(BF16)
    kv = jnp.dot(xn, w_ref[...], preferred_element_type=F32)
    tm = kv.shape[0]
    nrow = 2 * C_HEADS
    for j in range(nrow):
        x = kv[:, j * C_HEAD_DIM:(j + 1) * C_HEAD_DIM]
        if j < C_HEADS:
            x = _rms(x) * knc_ref[...]
        o_ref[pl.ds(j, tm, stride=nrow), :] = x


def _mem_kv(mem2d, mn, w, knc, tm):
    n = mem2d.shape[0]
    nrow = 2 * C_HEADS
    return pl.pallas_call(
        _mem_kv_body, grid=(n // tm,),
        in_specs=[pl.BlockSpec((tm, D_MODEL), lambda i: (i, 0)), _const_spec(mn.shape), _const_spec(w.shape),
                  _const_spec(knc.shape)],
        out_specs=pl.BlockSpec((tm * nrow, C_HEAD_DIM), lambda i: (i, 0)),
        out_shape=jax.ShapeDtypeStruct((n * nrow, C_HEAD_DIM), F32),
        compiler_params=_cparams(("parallel",)), name="mem_kv")(mem2d, mn, w, knc)


def _cross_body(q_ref, mem_ref, o_ref, *, bb):
    pairs = [(b, h) for b in range(bb) for h in range(C_HEADS)]
    scores = []
    for b, h in pairs:
        q = q_ref[b, :, h * C_HEAD_DIM:(h + 1) * C_HEAD_DIM].astype(BF16)
        kh = mem_ref[b, pl.ds(h, N_MEM, stride=2 * C_HEADS), :].astype(BF16)
        scores.append(lax.dot_general(q, kh, NT, preferred_element_type=F32))
    probs = []
    for s in scores:
        p = jnp.exp(s - jnp.max(s, axis=-1, keepdims=True))
        probs.append((p.astype(BF16), jnp.sum(p, axis=-1, keepdims=True)))
    for (b, h), (p, l) in zip(pairs, probs):
        vh = mem_ref[b, pl.ds(C_HEADS + h, N_MEM, stride=2 * C_HEADS), :].astype(BF16)
        o = jnp.dot(p, vh, preferred_element_type=F32) / l
        o_ref[b, :, h * C_HEAD_DIM:(h + 1) * C_HEAD_DIM] = o.astype(o_ref.dtype)


def _cross(qc, mem_kv, bb, tq):
    bsz, seq, _ = qc.shape
    return pl.pallas_call(
        functools.partial(_cross_body, bb=bb), grid=(bsz // bb, seq // tq),
        in_specs=[pl.BlockSpec((bb, tq, C_WIDTH), lambda i, t: (i, t, 0)),
                  pl.BlockSpec((bb,) + mem_kv.shape[1:], lambda i, t: (i, 0, 0))],
        out_specs=pl.BlockSpec((bb, tq, C_WIDTH), lambda i, t: (i, t, 0)),
        out_shape=jax.ShapeDtypeStruct((bsz, seq, C_WIDTH), F32),
        compiler_params=_cparams(("parallel", "arbitrary")), name="cross")(qc, mem_kv)


def _merge_body(x_ref, nm_ref, wgl, oa_ref, ob_ref, oc_ref, wba, wbb, wbc, wout, nf_ref, wr, br,
                h_o, xt_o, route_o):
    x = x_ref[...]
    xn = (_rms(x) * nm_ref[...]).astype(BF16)
    merged = None
    for i, (o_ref, w_ref) in enumerate(((oa_ref, wba), (ob_ref, wbb), (oc_ref, wbc))):
        gate = jax.nn.sigmoid(jnp.dot(xn, wgl[:, i * D_MODEL:(i + 1) * D_MODEL], preferred_element_type=F32))
        term = gate * jnp.dot(o_ref[...].astype(BF16), w_ref[...], preferred_element_type=F32)
        merged = term if merged is None else merged + term
    h = x + jnp.dot(merged.astype(BF16), wout[...], preferred_element_type=F32)
    h_o[...] = h
    xt = _rms(h) * nf_ref[...]
    xt_o[...] = xt
    xt_hi = xt.astype(BF16)
    xt_lo = (xt - xt_hi.astype(F32)).astype(BF16)
    logits = (jnp.dot(xt_hi, wr[0], preferred_element_type=F32) + jnp.dot(xt_hi, wr[1], preferred_element_type=F32)
              + jnp.dot(xt_lo, wr[0], preferred_element_type=F32) + br[...])
    lane = lax.broadcasted_iota(jnp.int32, logits.shape, 1)
    lane_f = lane.astype(F32)
    big = float(LANES)
    is_g = (lane >= N_EXPERTS) & (lane < N_EXPERTS + N_GROUPS_E)
    glog = jnp.where(is_g, logits, NEG)
    gmax = jnp.max(glog, axis=-1, keepdims=True)
    grp = jnp.min(jnp.where(glog == gmax, lane_f, big), axis=-1, keepdims=True) - float(N_EXPERTS)
    g_w = 1.0 / jnp.sum(jnp.where(is_g, jnp.exp(glog - gmax), 0.0), axis=-1, keepdims=True)
    in_grp = (lane < N_EXPERTS) & (jnp.floor(lane_f * (1.0 / E_PER_GROUP)) == grp)
    elog = jnp.where(in_grp, logits, NEG)
    v1 = jnp.max(elog, axis=-1, keepdims=True)
    i1 = jnp.min(jnp.where(elog == v1, lane_f, big), axis=-1, keepdims=True)
    elog2 = jnp.where(lane_f == i1, NEG, elog)
    v2 = jnp.max(elog2, axis=-1, keepdims=True)
    i2 = jnp.min(jnp.where(elog2 == v2, lane_f, big), axis=-1, keepdims=True)
    e2 = jnp.exp(v2 - v1)
    w1 = g_w / (1.0 + e2)
    w2 = g_w * e2 / (1.0 + e2)
    route_o[...] = jnp.where(lane == 0, i1, jnp.where(lane == 1, i2, jnp.where(lane == 2, w1,
                             jnp.where(lane == 3, w2, 0.0))))


def _merge(x2d, oa, ob, oc, wts, tm):
    n = x2d.shape[0]
    consts = [wts['nm'], wts['wgl']]
    consts2 = [wts['wba'], wts['wbb'], wts['wbc'], wts['wout'], wts['nf'], wts['wr'], wts['br']]

    def tile(w):
        return pl.BlockSpec((tm, w), lambda i: (i, 0))

    in_specs = ([tile(D_MODEL)] + [_const_spec(a.shape) for a in consts]
                + [tile(A_GW), tile(B_VW), tile(C_WIDTH)] + [_const_spec(a.shape) for a in consts2])
    return pl.pallas_call(
        _merge_body, grid=(n // tm,), in_specs=in_specs,
        out_specs=[tile(D_MODEL), tile(D_MODEL), tile(LANES)],
        out_shape=[jax.ShapeDtypeStruct((n, D_MODEL), F32), jax.ShapeDtypeStruct((n, D_MODEL), F32),
                   jax.ShapeDtypeStruct((n, LANES), F32)],
        compiler_params=_cparams(("parallel",)), name="merge")(x2d, *consts, oa, ob, oc, *consts2)


SUBLANES = 8


def _for_rows(n_rows, fn):
    def body(g, carry):
        for r in range(SUBLANES):
            fn(g, r)
        return carry
    lax.fori_loop(0, n_rows // SUBLANES, body, 0)


def _wait_rows(hbm, rows, sem):
    pltpu.make_async_copy(hbm.at[pl.ds(0, rows)], hbm.at[pl.ds(0, rows)], sem).wait()


def _dispatch_body(*refs, tm, tiles, n_zero, zero_steps):
    dest_ref, zslot_ref = refs[0], refs[1]
    xt_refs = refs[2:2 + len(tiles)]
    xs_out, stage, zrow, sem = refs[2 + len(tiles):]
    j = pl.program_id(0)
    nsteps = pl.num_programs(0)
    slot = j % 2

    def wait_step(step, s):
        _wait_rows(xs_out, 2 * tm, sem.at[s])

        @pl.when(step < zero_steps)
        def _():
            _wait_rows(xs_out, n_zero, sem.at[s])

    @pl.when(j >= 2)
    def _():
        wait_step(j - 2, slot)

    @pl.when(j == 0)
    def _():
        zrow[...] = jnp.zeros_like(zrow)

    first = 0
    for xt_ref, nt in zip(xt_refs, tiles):
        @pl.when(jnp.logical_and(j >= first, j < first + nt))
        def _(xt_ref=xt_ref):
            stage[slot] = xt_ref[...].reshape(stage.shape[1:])
        first += nt

    for k in range(2):
        def start_row(g, r, k=k):
            pltpu.make_async_copy(stage.at[slot, g, pl.ds(r, 1)],
                                  xs_out.at[pl.ds(dest_ref[0, k * tm + g * SUBLANES + r], 1)], sem.at[slot]).start()
        _for_rows(tm, start_row)

    @pl.when(j < zero_steps)
    def _():
        def start_zero(g, r):
            pltpu.make_async_copy(zrow.at[pl.ds(0, 1)], xs_out.at[pl.ds(zslot_ref[0, g * SUBLANES + r], 1)],
                                  sem.at[slot]).start()
        _for_rows(n_zero, start_zero)

    @pl.when(j == nsteps - 1)
    def _():
        wait_step(j, slot)

        @pl.when(j >= 1)
        def _():
            wait_step(j - 1, 1 - slot)


def _dispatch(xts, dest3, unused, n_slots, tm):
    tiles = [xt.shape[0] // tm for xt in xts]
    nsteps = sum(tiles)
    zero_steps = tiles[0]
    assert unused.shape[0] % (zero_steps * SUBLANES) == 0
    n_zero = unused.shape[0] // zero_steps
    in_specs = [pl.BlockSpec((None, 1, 2 * tm), lambda j: (j, 0, 0), memory_space=pltpu.SMEM),
                pl.BlockSpec((None, 1, n_zero), lambda j: (jnp.minimum(j, zero_steps - 1), 0, 0),
                             memory_space=pltpu.SMEM)]
    first = 0
    for nt in tiles:
        in_specs.append(pl.BlockSpec((tm, D_MODEL), lambda j, first=first, nt=nt: (jnp.clip(j - first, 0, nt - 1), 0)))
        first += nt
    return pl.pallas_call(
        functools.partial(_dispatch_body, tm=tm, tiles=tiles, n_zero=n_zero, zero_steps=zero_steps), grid=(nsteps,),
        in_specs=in_specs,
        out_specs=pl.BlockSpec(memory_space=pl.ANY),
        out_shape=jax.ShapeDtypeStruct((n_slots, D_MODEL), F32),
        scratch_shapes=[pltpu.VMEM((2, tm // SUBLANES, SUBLANES, D_MODEL), F32),
                        pltpu.VMEM((SUBLANES, D_MODEL), F32), pltpu.SemaphoreType.DMA((2,))],
        compiler_params=_cparams(("arbitrary",)), name="dispatch")(
            dest3, unused.reshape(zero_steps, 1, n_zero), *xts)


def _expert_body(be_ref, xs_ref, wg_ref, wu_ref, wd_ref, y_ref):
    del be_ref
    xb = xs_ref[...].astype(BF16)
    gate = jnp.dot(xb, wg_ref[0], preferred_element_type=F32)
    hid = gate * jax.nn.sigmoid(gate) * jnp.dot(xb, wu_ref[0], preferred_element_type=F32)
    y_ref[...] = jnp.dot(hid.astype(BF16), wd_ref[0], preferred_element_type=F32)


def _experts(xs, block_e, wg, wu, wd, bm):
    nblk = block_e.shape[0]
    up_spec = pl.BlockSpec((1, D_MODEL, D_EXPERT), lambda j, be: (be[j], 0, 0))
    grid_spec = pltpu.PrefetchScalarGridSpec(
        num_scalar_prefetch=1, grid=(nblk,),
        in_specs=[pl.BlockSpec((bm, D_MODEL), lambda j, be: (j, 0)), up_spec, up_spec,
                  pl.BlockSpec((1, D_EXPERT, D_MODEL), lambda j, be: (be[j], 0, 0))],
        out_specs=pl.BlockSpec((bm, D_MODEL), lambda j, be: (j, 0)))
    return pl.pallas_call(
        _expert_body, grid_spec=grid_spec,
        out_shape=jax.ShapeDtypeStruct((nblk * bm, D_MODEL), F32),
        compiler_params=_cparams(("parallel",)), name="experts")(block_e, xs, wg, wu, wd)


def _combine_body(d_cur, d_next, h_ref, route_ref, ys_hbm, y_ref, ybuf, sem, *, tm):
    j = pl.program_id(0)
    nblk = pl.num_programs(0)
    slot = j % 2

    def gather(idx_ref, s):
        def start_row(g, r):
            pltpu.make_async_copy(ys_hbm.at[pl.ds(idx_ref[0, g * SUBLANES + r], 1)], ybuf.at[s, g, pl.ds(r, 1)],
                                  sem.at[s]).start()
        _for_rows(2 * tm, start_row)

    @pl.when(j == 0)
    def _():
        gather(d_cur, 0)

    @pl.when(j + 1 < nblk)
    def _():
        gather(d_next, 1 - slot)

    _wait_rows(ys_hbm, 2 * tm, sem.at[slot])
    route = route_ref[...]
    w1 = route[:, 2:3]
    w2 = route[:, 3:4]
    ng = tm // SUBLANES
    y1 = ybuf[slot, 0:ng].reshape(tm, D_MODEL)
    y2 = ybuf[slot, ng:2 * ng].reshape(tm, D_MODEL)
    y_ref[...] = h_ref[...] + (y1 * w1 + y2 * w2)


def _tile_slots(dest, tm):
    nblk = dest.shape[1] // tm
    return dest.reshape(2, nblk, tm).transpose(1, 0, 2).reshape(nblk, 1, 2 * tm)


def _combine(h, route, ys, dest, tm):
    n = h.shape[0]
    nblk = n // tm
    d3 = _tile_slots(dest, tm)
    return pl.pallas_call(
        functools.partial(_combine_body, tm=tm), grid=(nblk,),
        in_specs=[
            pl.BlockSpec((None, 1, 2 * tm), lambda j: (j, 0, 0), memory_space=pltpu.SMEM),
            pl.BlockSpec((None, 1, 2 * tm), lambda j: (jnp.minimum(j + 1, nblk - 1), 0, 0), memory_space=pltpu.SMEM),
            pl.BlockSpec((tm, D_MODEL), lambda j: (j, 0)),
            pl.BlockSpec((tm, LANES), lambda j: (j, 0)),
            pl.BlockSpec(memory_space=pl.ANY),
        ],
        out_specs=pl.BlockSpec((tm, D_MODEL), lambda j: (j, 0)),
        out_shape=jax.ShapeDtypeStruct((n, D_MODEL), F32),
        scratch_shapes=[pltpu.VMEM((2, 2 * tm // SUBLANES, SUBLANES, D_MODEL), F32), pltpu.SemaphoreType.DMA((2,))],
        compiler_params=_cparams(("arbitrary",)), name="combine")(d3, d3, h, route, ys)


def _dispatch_plan(route, bm):
    n = route.shape[0]
    flat_e = jnp.concatenate([route[:, 0], route[:, 1]]).astype(jnp.int32)
    n_asg = flat_e.shape[0]
    onehot = (flat_e[:, None] == jnp.arange(N_EXPERTS, dtype=jnp.int32)[None, :]).astype(jnp.int32)
    csum = jnp.cumsum(onehot, axis=0)
    rank = jnp.sum(csum * onehot, axis=1) - 1
    counts = csum[-1]
    padded = (counts + bm - 1) // bm * bm
    pad_end = jnp.cumsum(padded)
    dest = jnp.sum((pad_end - padded)[None, :] * onehot, axis=1) + rank
    nblk = -(-n_asg // bm) + N_EXPERTS
    first_slot = jnp.arange(nblk, dtype=jnp.int32) * bm
    block_e = jnp.minimum(jnp.sum((pad_end[None, :] <= first_slot[:, None]).astype(jnp.int32), axis=1),
                          N_EXPERTS - 1)
    pad_cum = jnp.cumsum(padded - counts)
    i = jnp.arange(nblk * bm - n_asg, dtype=jnp.int32)
    seg = jnp.sum((pad_cum[None, :] <= i[:, None]).astype(jnp.int32), axis=1)
    seg_first_slot = jnp.concatenate([pad_end - padded + counts, pad_end[-1:]])
    seg_first_i = jnp.concatenate([jnp.zeros((1,), jnp.int32), pad_cum])
    unused = seg_first_slot[seg] + i - seg_first_i[seg]
    return dest.reshape(2, n), block_e, unused


def _hmoe(groups, wts, bm, tm):
    route_all = jnp.concatenate([route for _, _, route in groups], axis=0)
    dest_all, block_e, unused = _dispatch_plan(route_all, bm)
    xs = _dispatch([xt for _, xt, _ in groups], _tile_slots(dest_all, tm), unused, block_e.shape[0] * bm, tm)
    ys = _experts(xs, block_e, wts['wg'], wts['wu'], wts['wd'], bm)
    ys_out, off = [], 0
    for h, _, route in groups:
        n = h.shape[0]
        ys_out.append(_combine(h, route, ys, dest_all[:, off:off + n], tm))
        off += n
    return ys_out


def _prep_weights(norm_mix, w_in, qn_a, kn_a, qn_c, kn_c, gla_gate_up, gla_gate_bias, gla_norm, mem_norm,
                  w_mem_kv, w_branch_a, w_branch_b, w_branch_c, w_out, norm_ffn, w_router_group, b_router_group,
                  w_router_expert, b_router_expert, w_exp_gate, w_exp_up, w_exp_down):
    splits = (A_WIDTH, A_WIDTH, A_WIDTH, B_KW, B_KW, B_VW, B_VW, B_RANK, C_WIDTH, 3 * D_MODEL)
    offs = np.concatenate([[0], np.cumsum(splits)])
    names = ('wqa', 'wka', 'wva', 'wqb', 'wkb', 'wvb', 'wrb', 'wab', 'wqc', 'wgl')
    wts = {nm: w_in[:, int(offs[i]):int(offs[i + 1])].astype(BF16) for i, nm in enumerate(names)}
    wts['wab'] = jnp.pad(wts['wab'], ((0, 0), (0, LANES - B_RANK)))
    wts['gup'] = jnp.pad(gla_gate_up.astype(F32), ((0, LANES - B_RANK), (0, 0)))
    wts['gbias'] = gla_gate_bias.reshape(1, B_KW)
    wts['nm'] = norm_mix.reshape(1, D_MODEL)
    wts['qna'] = jnp.tile(qn_a, 2).reshape(1, LANES)
    wts['kna'] = jnp.tile(kn_a, 2).reshape(1, LANES)
    wts['qnc'] = qn_c.reshape(1, C_HEAD_DIM)
    wts['knc'] = kn_c.reshape(1, C_HEAD_DIM)
    wts['gn'] = gla_norm.reshape(1, B_DV)
    wts['mn'] = mem_norm.reshape(1, D_MODEL)
    wts['wmem'] = w_mem_kv.astype(BF16)
    wts['wba'] = w_branch_a.astype(BF16)
    wts['wbb'] = w_branch_b.astype(BF16)
    wts['wbc'] = w_branch_c.astype(BF16)
    wts['wout'] = w_out.astype(BF16)
    wts['nf'] = norm_ffn.reshape(1, D_MODEL)
    wr = jnp.concatenate([w_router_expert, w_router_group], axis=1)
    br = jnp.concatenate([b_router_expert, b_router_group])
    npad = LANES - N_EXPERTS - N_GROUPS_E
    wr = jnp.pad(wr, ((0, 0), (0, npad)))
    wr_hi = wr.astype(BF16)
    wts['wr'] = jnp.stack([wr_hi, (wr - wr_hi.astype(F32)).astype(BF16)])
    wts['br'] = jnp.pad(br, (0, npad)).reshape(1, LANES)
    wts['wg'] = w_exp_gate.astype(BF16)
    wts['wu'] = w_exp_up.astype(BF16)
    wts['wd'] = w_exp_down.astype(BF16)
    return wts


def _tile(n, want):
    t = min(n, want)
    assert n % t == 0
    return t


def _layer(x, wts, attend, gla_state, mem_kv, cfg):
    bsz, seq, _ = x.shape
    n = bsz * seq
    x2d = x.reshape(n, D_MODEL)
    qa, ka, va, qb, kb, vb, rb, la, qc = _in_proj(x2d, wts, _tile(n, cfg['tm_in']))

    def r3(a):
        return a.reshape(bsz, seq, a.shape[-1])

    def r4(a):
        return a.reshape(a.shape[0], bsz, seq, LANES)

    oa, extras = attend(r4(qa), r4(ka), r4(va))
    ob, s_fin = _gla(r3(qb), r3(kb), r3(vb), r3(la), r3(rb), wts['gn'], gla_state, cfg['gla_bb'],
                     _tile(seq, cfg['gla_tc']))
    oc = _cross(r3(qc), mem_kv, cfg['cross_bb'], _tile(seq, cfg['cross_tq']))
    pre_ffn = _merge(x2d, oa.reshape(n, A_GW), ob.reshape(n, B_VW), oc.reshape(n, C_WIDTH), wts,
                     _tile(n, cfg['tm_merge']))
    return pre_ffn, extras, s_fin


def kernel(x_prompt, x_sample, mem_prompt, cache_win1, cache_win2, cache_win3, state_gla, cache_mem, norm_mix, w_in, qn_a, kn_a, qn_c, kn_c, gla_gate_up, gla_gate_bias, gla_norm, mem_norm, w_mem_kv, w_branch_a, w_branch_b, w_branch_c, w_out, norm_ffn, w_router_group, b_router_group, w_router_expert, b_router_expert, w_exp_gate, w_exp_up, w_exp_down):
    wts = _prep_weights(norm_mix, w_in, qn_a, kn_a, qn_c, kn_c, gla_gate_up, gla_gate_bias, gla_norm, mem_norm,
                        w_mem_kv, w_branch_a, w_branch_b, w_branch_c, w_out, norm_ffn, w_router_group,
                        b_router_group, w_router_expert, b_router_expert, w_exp_gate, w_exp_up, w_exp_down)
    bsz, seq, _ = x_prompt.shape
    bd, n_new, _ = x_sample.shape

    mem_p = _mem_kv(mem_prompt.reshape(bsz * N_MEM, D_MODEL), wts['mn'], wts['wmem'], wts['knc'], 256)
    mem_p = mem_p.reshape(bsz, N_MEM * 2 * C_HEADS, C_HEAD_DIM)

    cfg_p = dict(tm_in=256, tm_merge=512, gla_bb=2, gla_tc=256, cross_bb=1, cross_tq=512)
    s0 = jnp.zeros((bsz, B_HEADS, B_DK, B_DV), F32)
    ffn_in_p, wins_p, gla_p = _layer(x_prompt, wts, _dilated_prompt, s0, mem_p, cfg_p)

    def attend_sample(qa, ka, va):
        return _dilated_sample(qa, ka, va, (cache_win1, cache_win2, cache_win3), bb=1)

    cfg_s = dict(tm_in=256, tm_merge=256, gla_bb=8, gla_tc=n_new, cross_bb=8, cross_tq=n_new)
    ffn_in_s, wins_s, gla_s = _layer(x_sample, wts, attend_sample, state_gla,
                                     cache_mem.reshape(bd, N_MEM * 2 * C_HEADS, C_HEAD_DIM), cfg_s)

    y_p, y_s = _hmoe([ffn_in_p, ffn_in_s], wts, bm=256, tm=256)
    y_p = y_p.reshape(x_prompt.shape)
    y_s = y_s.reshape(x_sample.shape)

    return (y_p, y_s, wins_p[0], wins_p[1], wins_p[2], gla_p, mem_p.reshape(bsz, N_MEM, 2, C_HEADS, C_HEAD_DIM),
            wins_s[0], wins_s[1], wins_s[2], gla_s)
```

```python
import functools

import jax
import jax.numpy as jnp
import numpy as np
from jax import lax
from jax.experimental import pallas as pl
from jax.experimental.pallas import tpu as pltpu

F32 = jnp.float32
BF16 = jnp.bfloat16
HIGHEST = lax.Precision.HIGHEST

D_MODEL = 1024
A_GROUPS = ((128, 1), (512, 4), (2048, 16))
A_J = 128
A_HEADS = 4
A_HEAD_DIM = 64
A_GW = A_HEADS * A_HEAD_DIM
A_WIDTH = 3 * A_GW
B_HEADS = 4
B_DK = 64
B_DV = 128
B_KW = B_HEADS * B_DK
B_VW = B_HEADS * B_DV
B_RANK = 16
B_TAU = 16.0
B_CHUNK = 64
C_HEADS = 4
C_HEAD_DIM = 128
C_WIDTH = C_HEADS * C_HEAD_DIM
N_MEM = 256
N_GROUPS_E = 4
E_PER_GROUP = 8
N_EXPERTS = 32
D_EXPERT = 256
EPS = 1e-6
NEG = -1e30
LOG2E = 1.4426950408889634
LANES = 128
VMEM_LIMIT = 56 * 1024 * 1024

NT = (((1,), (1,)), ((), ()))
TN = (((0,), (0,)), ((), ()))


def _cparams(sem):
    return pltpu.CompilerParams(dimension_semantics=sem, vmem_limit_bytes=VMEM_LIMIT)


def _const_spec(shape, single=True):
    nd = len(shape)
    if single:
        return pl.BlockSpec(shape, lambda *_: (0,) * nd, pipeline_mode=pl.Buffered(1))
    return pl.BlockSpec(shape, lambda *_: (0,) * nd)


def _rms(x):
    return x * lax.rsqrt(jnp.mean(x * x, axis=-1, keepdims=True) + EPS)


def _headnorm64(h, g128, scale, o_ref):
    lo = lax.broadcasted_iota(jnp.int32, (1, LANES), 1) < A_HEAD_DIM
    for c in range(h.shape[1] // LANES):
        x = h[:, c * LANES:(c + 1) * LANES]
        x2 = x * x
        s_lo = jnp.sum(jnp.where(lo, x2, 0.0), axis=-1, keepdims=True)
        s_hi = jnp.sum(jnp.where(lo, 0.0, x2), axis=-1, keepdims=True)
        ms = jnp.where(lo, s_lo, s_hi) * (1.0 / A_HEAD_DIM)
        y = x * lax.rsqrt(ms + EPS) * g128
        if scale != 1.0:
            y = y * scale
        o_ref[c] = y.astype(o_ref.dtype)


def _in_proj_body(x_ref, nm_ref, wqa, wka, wva, wqb, wkb, wvb, wrb, wab, wqc, qna, kna, qnc, gup, gbias,
                  qa_o, ka_o, va_o, qb_o, kb_o, vb_o, rb_o, la_o, qc_o):
    xn = (_rms(x_ref[...]) * nm_ref[...]).astype(BF16)

    def mm(w):
        return jnp.dot(xn, w[...], preferred_element_type=F32)

    _headnorm64(mm(wqa), qna[...], A_HEAD_DIM ** -0.5, qa_o)
    _headnorm64(mm(wka), kna[...], 1.0, ka_o)
    hv = mm(wva)
    for c in range(A_WIDTH // LANES):
        va_o[c] = hv[:, c * LANES:(c + 1) * LANES]
    qb_o[...] = mm(wqb) * (B_DK ** -0.5)
    kb_o[...] = mm(wkb)
    vb_o[...] = mm(wvb)
    rb_o[...] = mm(wrb)
    z = jnp.dot(mm(wab), gup[...], precision=HIGHEST, preferred_element_type=F32) + gbias[...]
    la_o[...] = (jnp.minimum(z, 0.0) - jnp.log1p(jnp.exp(-jnp.abs(z)))) * (1.0 / B_TAU)
    hq = mm(wqc)
    for h in range(C_HEADS):
        sl = slice(h * C_HEAD_DIM, (h + 1) * C_HEAD_DIM)
        qc_o[:, sl] = _rms(hq[:, sl]) * qnc[...] * (C_HEAD_DIM ** -0.5)


def _in_proj(x2d, wts, tm):
    n = x2d.shape[0]
    ins = [x2d, wts['nm'], wts['wqa'], wts['wka'], wts['wva'], wts['wqb'], wts['wkb'], wts['wvb'], wts['wrb'],
           wts['wab'], wts['wqc'], wts['qna'], wts['kna'], wts['qnc'], wts['gup'], wts['gbias']]
    in_specs = [pl.BlockSpec((tm, D_MODEL), lambda i: (i, 0))] + [_const_spec(a.shape, single=False) for a in ins[1:]]
    nch = A_WIDTH // LANES
    widths = [B_KW, B_KW, B_VW, B_VW, B_KW, C_WIDTH]
    out_shape = ([jax.ShapeDtypeStruct((nch, n, LANES), F32)] * 3
                 + [jax.ShapeDtypeStruct((n, w), F32) for w in widths])
    out_specs = ([pl.BlockSpec((nch, tm, LANES), lambda i: (0, i, 0))] * 3
                 + [pl.BlockSpec((tm, w), lambda i: (i, 0)) for w in widths])
    return pl.pallas_call(
        _in_proj_body, grid=(n // tm,), in_specs=in_specs, out_specs=out_specs, out_shape=out_shape,
        compiler_params=_cparams(("parallel",)), name="in_proj")(*ins)


def _dil_prompt_body(q_ref, k_ref, v_ref, o_ref, w1_ref, w2_ref, w3_ref, os_ref, ms_ref, ls_ref, *, seq):
    g = pl.program_id(1)
    win_refs = (w1_ref, w2_ref, w3_ref)
    J = A_J
    row = lax.broadcasted_iota(jnp.int32, (J, J), 0)
    col = lax.broadcasted_iota(jnp.int32, (J, J), 1)
    cur_ok = col <= row
    lo = lax.broadcasted_iota(jnp.int32, (1, LANES), 1) < A_HEAD_DIM
    hi = jnp.logical_not(lo)

    for gi, (_, r) in enumerate(A_GROUPS):
        nb = seq // (r * J)

        @pl.when(g == gi)
        def _(gi=gi, r=r, nb=nb):
            def blk(i, carry):
                rho = i >> (nb.bit_length() - 1)
                n = i & (nb - 1)
                start = rho + n * (r * J)
                pstart = jnp.maximum(start - r * J, 0)
                if r == 1:
                    start, pstart = pl.multiple_of(start, J), pl.multiple_of(pstart, J)
                prev_ok = col >= row + jnp.where(n > 0, 0, J)

                def ld(ref, c, s):
                    return ref[c, 0, pl.ds(s, J, stride=r), :]

                heads = []
                for c in range(2):
                    q = ld(q_ref, c, start) * LOG2E
                    kc, vc = ld(k_ref, c, start).astype(BF16), ld(v_ref, c, start).astype(BF16)
                    kp, vp = ld(k_ref, c, pstart).astype(BF16), ld(v_ref, c, pstart).astype(BF16)
                    for lanes_ok in (lo, hi):
                        qh = jnp.where(lanes_ok, q, 0.0).astype(BF16)
                        s_c = lax.dot_general(qh, kc, NT, preferred_element_type=F32)
                        s_p = lax.dot_general(qh, kp, NT, preferred_element_type=F32)
                        heads.append((s_c, s_p, vc, vp))
                outs = []
                for s_c, s_p, vc, vp in heads:
                    s_c = jnp.where(cur_ok, s_c, NEG)
                    s_p = jnp.where(prev_ok, s_p, NEG)
                    m = jnp.max(jnp.maximum(s_c, s_p), axis=-1, keepdims=True)
                    p_c = jnp.exp2(s_c - m)
                    p_p = jnp.exp2(s_p - m)
                    l = jnp.sum(p_c + p_p, axis=-1, keepdims=True)
                    pv = (jnp.dot(p_c.astype(BF16), vc, preferred_element_type=F32)
                          + jnp.dot(p_p.astype(BF16), vp, preferred_element_type=F32))
                    outs.append((pv, m, l))
                for c in range(2):
                    (o0, m0, l0), (o1, m1, l1) = outs[2 * c], outs[2 * c + 1]
                    os_ref[gi, c, pl.ds(start, J, stride=r), :] = jnp.where(lo, o0, o1)
                    ms_ref[gi, c, pl.ds(start, J, stride=r), :] = jnp.where(lo, m0, m1)
                    ls_ref[gi, c, pl.ds(start, J, stride=r), :] = jnp.where(lo, l0, l1)
                return carry

            lax.fori_loop(0, seq // J, blk, 0, unroll=4)

            wb = win_refs[gi].shape[-1]
            for c in range(2):
                rows = slice(c * LANES, (c + 1) * LANES)
                win_refs[gi][0, rows, :] = k_ref[c, 0, seq - wb:seq, :].T
                win_refs[gi][0, A_GW + c * LANES:A_GW + (c + 1) * LANES, :] = v_ref[c, 0, seq - wb:seq, :].T

    @pl.when(g == len(A_GROUPS) - 1)
    def _():
        def comb(i, carry):
            sl = pl.ds(pl.multiple_of(i * J, J), J)
            for c in range(2):
                m0, m1, m2 = ms_ref[0, c, sl, :], ms_ref[1, c, sl, :], ms_ref[2, c, sl, :]
                mx = jnp.maximum(jnp.maximum(m0, m1), m2)
                w0, w1, w2 = jnp.exp2(m0 - mx), jnp.exp2(m1 - mx), jnp.exp2(m2 - mx)
                num = w0 * os_ref[0, c, sl, :] + w1 * os_ref[1, c, sl, :] + w2 * os_ref[2, c, sl, :]
                den = w0 * ls_ref[0, c, sl, :] + w1 * ls_ref[1, c, sl, :] + w2 * ls_ref[2, c, sl, :]
                o_ref[0, sl, c * LANES:(c + 1) * LANES] = (num / den).astype(o_ref.dtype)
            return carry

        lax.fori_loop(0, seq // J, comb, 0)


def _dilated_prompt(qa, ka, va):
    _, bsz, seq, _ = qa.shape
    ng = len(A_GROUPS)
    spec = pl.BlockSpec((2, 1, seq, LANES), lambda b, g: (g, b, 0, 0))
    wbs = [min(w, seq) for w, _ in A_GROUPS]
    oa, *wins = pl.pallas_call(
        functools.partial(_dil_prompt_body, seq=seq), grid=(bsz, ng),
        in_specs=[spec, spec, spec],
        out_specs=[pl.BlockSpec((1, seq, A_GW), lambda b, g: (b, 0, 0))]
        + [pl.BlockSpec((1, 2 * A_GW, wb), lambda b, g: (b, 0, 0)) for wb in wbs],
        out_shape=[jax.ShapeDtypeStruct((bsz, seq, A_GW), BF16)]
        + [jax.ShapeDtypeStruct((bsz, 2 * A_GW, wb), F32) for wb in wbs],
        scratch_shapes=[pltpu.VMEM((ng, 2, seq, LANES), F32)] * 3,
        compiler_params=_cparams(("parallel", "arbitrary")), name="dilated_prompt")(qa, ka, va)
    return oa, [w.reshape(bsz, 2, A_HEADS, A_HEAD_DIM, w.shape[-1]).transpose(0, 4, 1, 2, 3) for w in wins]


def _dil_sample_body(q_ref, k_ref, v_ref, c1_ref, c2_ref, c3_ref, o_ref, n1_ref, n2_ref, n3_ref, *, bb, n_new):
    L = n_new
    caches = (c1_ref, c2_ref, c3_ref)
    outs = (n1_ref, n2_ref, n3_ref)
    nq = A_HEADS * L
    head_of_lane = lax.broadcasted_iota(jnp.int32, (1, A_GW), 1) >> (A_HEAD_DIM.bit_length() - 1)
    q_of_row = lax.broadcasted_iota(jnp.int32, (nq, 1), 0) & (L - 1)
    t_new = lax.broadcasted_iota(jnp.int32, (nq, L), 1)
    lane = lax.broadcasted_iota(jnp.int32, (1, LANES), 1)
    pad_rows = jnp.zeros((LANES - L, 2 * A_GW), F32)

    def per_head_rows(x):
        return jnp.concatenate([jnp.where(head_of_lane == h, x, 0.0) for h in range(A_HEADS)], axis=0)

    for b in range(bb):
        scores, scores_n, vts, vns = [], [], [], []
        for gi, (w, r) in enumerate(A_GROUPS):
            cref = caches[gi]
            qg, kn, vn = [jnp.concatenate([ref[2 * gi, b], ref[2 * gi + 1, b]], axis=-1)
                          for ref in (q_ref, k_ref, v_ref)]
            qbd = per_head_rows(qg).astype(BF16)
            kt = cref[b, 0:A_GW, :].astype(BF16)
            s = jnp.dot(qbd, kt, preferred_element_type=F32)
            s_n = lax.dot_general(qbd, kn.astype(BF16), NT, preferred_element_type=F32)
            pos = lax.broadcasted_iota(jnp.int32, (nq, w), 1)
            s = jnp.where((pos >= q_of_row) & (((q_of_row - pos) & (r - 1)) == 0), s, NEG)
            s_n = jnp.where((t_new <= q_of_row) & (((q_of_row - t_new) & (r - 1)) == 0), s_n, NEG)
            scores.append(s)
            scores_n.append(s_n)
            vts.append(cref[b, A_GW:2 * A_GW, :].astype(BF16))
            vns.append(vn.astype(BF16))

            rolled = pltpu.roll(cref[b], w - L, axis=1)
            outs[gi][b] = rolled
            new_t = jnp.concatenate([jnp.concatenate([kn, vn], axis=-1), pad_rows], axis=0).T
            outs[gi][b, :, w - LANES:w] = jnp.where(lane < LANES - L, rolled[:, w - LANES:w],
                                                    pltpu.roll(new_t, LANES - L, axis=1))

        m = None
        for s in scores + scores_n:
            ms = jnp.max(s, axis=-1, keepdims=True)
            m = ms if m is None else jnp.maximum(m, ms)
        den = 0.0
        acc = jnp.zeros((nq, A_GW), F32)
        for s, s_n, vt, vn in zip(scores, scores_n, vts, vns):
            p = jnp.exp(s - m)
            p_n = jnp.exp(s_n - m)
            den = den + jnp.sum(p, axis=-1, keepdims=True) + jnp.sum(p_n, axis=-1, keepdims=True)
            acc = acc + lax.dot_general(p.astype(BF16), vt, NT, preferred_element_type=F32)
            acc = acc + jnp.dot(p_n.astype(BF16), vn, preferred_element_type=F32)
        acc = acc / den
        o = jnp.zeros((L, A_GW), F32)
        for h in range(A_HEADS):
            o = o + jnp.where(head_of_lane == h, acc[h * L:(h + 1) * L, :], 0.0)
        o_ref[b] = o.astype(o_ref.dtype)


def _dilated_sample(qa, ka, va, caches, bb):
    _, bd, n_new, _ = qa.shape
    cts = [c.transpose(0, 2, 3, 4, 1).reshape(bd, 2 * A_GW, c.shape[1]) for c in caches]
    new_spec = pl.BlockSpec((A_WIDTH // LANES, bb, n_new, LANES), lambda i: (0, i, 0, 0))
    cspecs = [pl.BlockSpec((bb,) + c.shape[1:], lambda i: (i, 0, 0)) for c in cts]
    oa, *nts = pl.pallas_call(
        functools.partial(_dil_sample_body, bb=bb, n_new=n_new), grid=(bd // bb,),
        in_specs=[new_spec, new_spec, new_spec] + cspecs,
        out_specs=[pl.BlockSpec((bb, n_new, A_GW), lambda i: (i, 0, 0))] + cspecs,
        out_shape=[jax.ShapeDtypeStruct((bd, n_new, A_GW), F32)] + [jax.ShapeDtypeStruct(c.shape, F32) for c in cts],
        compiler_params=_cparams(("parallel",)), name="dilated_sample")(qa, ka, va, *cts)
    news = [n.reshape(bd, 2, A_HEADS, A_HEAD_DIM, n.shape[-1]).transpose(0, 4, 1, 2, 3) for n in nts]
    return oa, news


def _gla_body(q_ref, k_ref, v_ref, la_ref, rb_ref, gn_ref, s0_ref, o_ref, sf_ref, s_ref, *, bb, tc, chunk):
    t = pl.program_id(1)
    C = chunk

    @pl.when(t == 0)
    def _():
        s_ref[...] = s0_ref[...]

    causal = lax.broadcasted_iota(jnp.int32, (C, C), 0) >= lax.broadcasted_iota(jnp.int32, (C, C), 1)
    lo = lax.broadcasted_iota(jnp.int32, (1, LANES), 1) < B_DK
    head_lanes = (lo, jnp.logical_not(lo))
    row_s = lax.broadcasted_iota(jnp.int32, (LANES, LANES), 0)
    eye = row_s == lax.broadcasted_iota(jnp.int32, (LANES, LANES), 1)
    top = row_s < B_DK
    pos = lax.broadcasted_iota(jnp.int32, (tc, 1), 0) & (C - 1)
    mid = C // 2 - 1
    gn = gn_ref[...]
    nc = tc // C

    def per_chunk_row(x, r):
        return jnp.concatenate([jnp.broadcast_to(x[c * C + r:c * C + r + 1, :], (C, x.shape[1])) for c in range(nc)],
                               axis=0)

    for b in range(bb):
        cum = la_ref[b]
        step = 1
        while step < C:
            cum = cum + jnp.where(pos >= step, pltpu.roll(cum, step, axis=0), 0.0)
            step *= 2
        cmid = per_chunk_row(cum, mid)
        clast = per_chunk_row(cum, C - 1)
        q = q_ref[b]
        k = k_ref[b]
        q_in = q * jnp.exp(cum)
        q_t = q * jnp.exp(cum - cmid)
        k_t = (k * jnp.exp(cmid - cum)).astype(BF16)
        k_l = (k * jnp.exp(clast - cum)).astype(BF16)
        e_last = jnp.exp(clast)

        o_intra = {}
        upd = {}
        for c in range(nc):
            rows = slice(c * C, (c + 1) * C)
            for p in range(B_HEADS // 2):
                pl_ = slice(p * LANES, (p + 1) * LANES)
                kl = k_l[rows, pl_]
                tn = []
                for hh in range(2):
                    h = 2 * p + hh
                    v = v_ref[b, rows, h * B_DV:(h + 1) * B_DV].astype(BF16)
                    qt = jnp.where(head_lanes[hh], q_t[rows, pl_], 0.0).astype(BF16)
                    att = lax.dot_general(qt, k_t[rows, pl_], NT, preferred_element_type=F32)
                    att = jnp.where(causal, att, 0.0).astype(BF16)
                    o_intra[c, h] = jnp.dot(att, v, preferred_element_type=F32)
                    tn.append(lax.dot_general(kl, v, TN, preferred_element_type=F32))
                upd[c, p] = jnp.where(top, tn[0], tn[1])

        for c in range(nc):
            rows = slice(c * C, (c + 1) * C)
            for p in range(B_HEADS // 2):
                pl_ = slice(p * LANES, (p + 1) * LANES)
                S = s_ref[b, p]
                S16 = S.astype(BF16)
                e_col = jnp.sum(jnp.where(eye, e_last[c * C:c * C + 1, pl_], 0.0), axis=-1, keepdims=True)
                s_ref[b, p] = e_col * S + upd[c, p]
                for hh in range(2):
                    h = 2 * p + hh
                    vs = slice(h * B_DV, (h + 1) * B_DV)
                    qi = jnp.where(head_lanes[hh], q_in[rows, pl_], 0.0).astype(BF16)
                    o = o_intra[c, h] + jnp.dot(qi, S16, preferred_element_type=F32)
                    mu = jnp.mean(o, axis=-1, keepdims=True)
                    d = o - mu
                    var = jnp.mean(d * d, axis=-1, keepdims=True)
                    rb = rb_ref[b, rows, vs]
                    o_ref[b, rows, vs] = (d * lax.rsqrt(var + EPS) * gn * (rb * jax.nn.sigmoid(rb))).astype(o_ref.dtype)

    @pl.when(t == pl.num_programs(1) - 1)
    def _():
        sf_ref[...] = s_ref[...]


def _gla(qb, kb, vb, la, rb, gn, s0, bb, tc):
    bsz, seq, _ = qb.shape
    chunk = int(np.gcd(seq, B_CHUNK))
    kspec = pl.BlockSpec((bb, tc, B_KW), lambda i, t: (i, t, 0))
    vspec = pl.BlockSpec((bb, tc, B_VW), lambda i, t: (i, t, 0))
    pair_shape = (B_HEADS // 2, 2 * B_DK, B_DV)
    sspec = pl.BlockSpec((bb,) + pair_shape, lambda i, t: (i, 0, 0, 0))
    o, s_fin = pl.pallas_call(
        functools.partial(_gla_body, bb=bb, tc=tc, chunk=chunk), grid=(bsz // bb, seq // tc),
        in_specs=[kspec, kspec, vspec, kspec, vspec, _const_spec(gn.shape), sspec],
        out_specs=[vspec, sspec],
        out_shape=[jax.ShapeDtypeStruct((bsz, seq, B_VW), BF16), jax.ShapeDtypeStruct((bsz,) + pair_shape, F32)],
        scratch_shapes=[pltpu.VMEM((bb,) + pair_shape, F32)],
        compiler_params=_cparams(("parallel", "arbitrary")), name="gla")(
            qb, kb, vb, la, rb, gn, s0.reshape((bsz,) + pair_shape))
    return o, s_fin.reshape(s0.shape)


def _mem_kv_body(m_ref, mn_ref, w_ref, knc_ref, o_ref):
    xn = (_rms(m_ref[...]) * mn_ref[...]).astype(BF16)
    kv = jnp.dot(xn, w_ref[...], preferred_element_type=F32)
    tm = kv.shape[0]
    nrow = 2 * C_HEADS
    for j in range(nrow):
        x = kv[:, j * C_HEAD_DIM:(j + 1) * C_HEAD_DIM]
        if j < C_HEADS:
            x = _rms(x) * knc_ref[...]
        o_ref[pl.ds(j, tm, stride=nrow), :] = x


def _mem_kv(mem2d, mn, w, knc, tm):
    n = mem2d.shape[0]
    nrow = 2 * C_HEADS
    return pl.pallas_call(
        _mem_kv_body, grid=(n // tm,),
        in_specs=[pl.BlockSpec((tm, D_MODEL), lambda i: (i, 0)), _const_spec(mn.shape), _const_spec(w.shape),
                  _const_spec(knc.shape)],
        out_specs=pl.BlockSpec((tm * nrow, C_HEAD_DIM), lambda i: (i, 0)),
        out_shape=jax.ShapeDtypeStruct((n * nrow, C_HEAD_DIM), F32),
        compiler_params=_cparams(("parallel",)), name="mem_kv")(mem2d, mn, w, knc)


def _cross_body(q_ref, mem_ref, o_ref, *, bb):
    pairs = [(b, h) for b in range(bb) for h in range(C_HEADS)]
    scores = []
    for b, h in pairs:
        q = q_ref[b, :, h * C_HEAD_DIM:(h + 1) * C_HEAD_DIM].astype(BF16)
        kh = mem_ref[b, pl.ds(h, N_MEM, stride=2 * C_HEADS), :].astype(BF16)
        scores.append(lax.dot_general(q, kh, NT, preferred_element_type=F32))
    probs = []
    for s in scores:
        p = jnp.exp(s - jnp.max(s, axis=-1, keepdims=True))
        probs.append((p.astype(BF16), jnp.sum(p, axis=-1, keepdims=True)))
    for (b, h), (p, l) in zip(pairs, probs):
        vh = mem_ref[b, pl.ds(C_HEADS + h, N_MEM, stride=2 * C_HEADS), :].astype(BF16)
        o = jnp.dot(p, vh, preferred_element_type=F32) / l
        o_ref[b, :, h * C_HEAD_DIM:(h + 1) * C_HEAD_DIM] = o.astype(o_ref.dtype)


def _cross(qc, mem_kv, bb, tq):
    bsz, seq, _ = qc.shape
    return pl.pallas_call(
        functools.partial(_cross_body, bb=bb), grid=(bsz // bb, seq // tq),
        in_specs=[pl.BlockSpec((bb, tq, C_WIDTH), lambda i, t: (i, t, 0)),
                  pl.BlockSpec((bb,) + mem_kv.shape[1:], lambda i, t: (i, 0, 0))],
        out_specs=pl.BlockSpec((bb, tq, C_WIDTH), lambda i, t: (i, t, 0)),
        out_shape=jax.ShapeDtypeStruct((bsz, seq, C_WIDTH), F32),
        compiler_params=_cparams(("parallel", "arbitrary")), name="cross")(qc, mem_kv)


def _merge_body(x_ref, nm_ref, wgl, oa_ref, ob_ref, oc_ref, wba, wbb, wbc, wout, nf_ref, wr, br,
                h_o, xt_o, route_o):
    x = x_ref[...]
    xn = (_rms(x) * nm_ref[...]).astype(BF16)
    merged = None
    for i, (o_ref, w_ref) in enumerate(((oa_ref, wba), (ob_ref, wbb), (oc_ref, wbc))):
        gate = jax.nn.sigmoid(jnp.dot(xn, wgl[:, i * D_MODEL:(i + 1) * D_MODEL], preferred_element_type=F32))
        term = gate * jnp.dot(o_ref[...].astype(BF16), w_ref[...], preferred_element_type=F32)
        merged = term if merged is None else merged + term
    h = x + jnp.dot(merged.astype(BF16), wout[...], preferred_element_type=F32)
    h_o[...] = h
    xt = _rms(h) * nf_ref[...]
    xt_o[...] = xt
    xt_hi = xt.astype(BF16)
    xt_lo = (xt - xt_hi.astype(F32)).astype(BF16)
    logits = (jnp.dot(xt_hi, wr[0], preferred_element_type=F32) + jnp.dot(xt_hi, wr[1], preferred_element_type=F32)
              + jnp.dot(xt_lo, wr[0], preferred_element_type=F32) + br[...])
    lane = lax.broadcasted_iota(jnp.int32, logits.shape, 1)
    lane_f = lane.astype(F32)
    big = float(LANES)
    is_g = (lane >= N_EXPERTS) & (lane < N_EXPERTS + N_GROUPS_E)
    glog = jnp.where(is_g, logits, NEG)
    gmax = jnp.max(glog, axis=-1, keepdims=True)
    grp = jnp.min(jnp.where(glog == gmax, lane_f, big), axis=-1, keepdims=True) - float(N_EXPERTS)
    g_w = 1.0 / jnp.sum(jnp.where(is_g, jnp.exp(glog - gmax), 0.0), axis=-1, keepdims=True)
    in_grp = (lane < N_EXPERTS) & (jnp.floor(lane_f * (1.0 / E_PER_GROUP)) == grp)
    elog = jnp.where(in_grp, logits, NEG)
    v1 = jnp.max(elog, axis=-1, keepdims=True)
    i1 = jnp.min(jnp.where(elog == v1, lane_f, big), axis=-1, keepdims=True)
    elog2 = jnp.where(lane_f == i1, NEG, elog)
    v2 = jnp.max(elog2, axis=-1, keepdims=True)
    i2 = jnp.min(jnp.where(elog2 == v2, lane_f, big), axis=-1, keepdims=True)
    e2 = jnp.exp(v2 - v1)
    w1 = g_w / (1.0 + e2)
    w2 = g_w * e2 / (1.0 + e2)
    route_o[...] = jnp.where(lane == 0, i1, jnp.where(lane == 1, i2, jnp.where(lane == 2, w1,
                             jnp.where(lane == 3, w2, 0.0))))


def _merge(x2d, oa, ob, oc, wts, tm):
    n = x2d.shape[0]
    consts = [wts['nm'], wts['wgl']]
    consts2 = [wts['wba'], wts['wbb'], wts['wbc'], wts['wout'], wts['nf'], wts['wr'], wts['br']]

    def tile(w):
        return pl.BlockSpec((tm, w), lambda i: (i, 0))

    in_specs = ([tile(D_MODEL)] + [_const_spec(a.shape) for a in consts]
                + [tile(A_GW), tile(B_VW), tile(C_WIDTH)] + [_const_spec(a.shape) for a in consts2])
    return pl.pallas_call(
        _merge_body, grid=(n // tm,), in_specs=in_specs,
        out_specs=[tile(D_MODEL), tile(D_MODEL), tile(LANES)],
        out_shape=[jax.ShapeDtypeStruct((n, D_MODEL), F32), jax.ShapeDtypeStruct((n, D_MODEL), F32),
                   jax.ShapeDtypeStruct((n, LANES), F32)],
        compiler_params=_cparams(("parallel",)), name="merge")(x2d, *consts, oa, ob, oc, *consts2)


SUBLANES = 8


def _for_rows(n_rows, fn):
    def body(g, carry):
        for r in range(SUBLANES):
            fn(g, r)
        return carry
    lax.fori_loop(0, n_rows // SUBLANES, body, 0)


def _wait_rows(hbm, rows, sem):
    pltpu.make_async_copy(hbm.at[pl.ds(0, rows)], hbm.at[pl.ds(0, rows)], sem).wait()


def _dispatch_body(*refs, tm, tiles, n_zero, zero_steps):
    dest_ref, zslot_ref = refs[0], refs[1]
    xt_refs = refs[2:2 + len(tiles)]
    xs_out, stage, zrow, sem = refs[2 + len(tiles):]
    j = pl.program_id(0)
    nsteps = pl.num_programs(0)
    slot = j % 2

    def wait_step(step, s):
        _wait_rows(xs_out, 2 * tm, sem.at[s])

        @pl.when(step < zero_steps)
        def _():
            _wait_rows(xs_out, n_zero, sem.at[s])

    @pl.when(j >= 2)
    def _():
        wait_step(j - 2, slot)

    @pl.when(j == 0)
    def _():
        zrow[...] = jnp.zeros_like(zrow)

    first = 0
    for xt_ref, nt in zip(xt_refs, tiles):
        @pl.when(jnp.logical_and(j >= first, j < first + nt))
        def _(xt_ref=xt_ref):
            stage[slot] = xt_ref[...].reshape(stage.shape[1:])
        first += nt

    for k in range(2):
        def start_row(g, r, k=k):
            pltpu.make_async_copy(stage.at[slot, g, pl.ds(r, 1)],
                                  xs_out.at[pl.ds(dest_ref[0, k * tm + g * SUBLANES + r], 1)], sem.at[slot]).start()
        _for_rows(tm, start_row)

    @pl.when(j < zero_steps)
    def _():
        def start_zero(g, r):
            pltpu.make_async_copy(zrow.at[pl.ds(0, 1)], xs_out.at[pl.ds(zslot_ref[0, g * SUBLANES + r], 1)],
                                  sem.at[slot]).start()
        _for_rows(n_zero, start_zero)

    @pl.when(j == nsteps - 1)
    def _():
        wait_step(j, slot)

        @pl.when(j >= 1)
        def _():
            wait_step(j - 1, 1 - slot)


def _dispatch(xts, dest3, unused, n_slots, tm):
    tiles = [xt.shape[0] // tm for xt in xts]
    nsteps = sum(tiles)
    zero_steps = tiles[0]
    assert unused.shape[0] % (zero_steps * SUBLANES) == 0
    n_zero = unused.shape[0] // zero_steps
    in_specs = [pl.BlockSpec((None, 1, 2 * tm), lambda j: (j, 0, 0), memory_space=pltpu.SMEM),
                pl.BlockSpec((None, 1, n_zero), lambda j: (jnp.minimum(j, zero_steps - 1), 0, 0),
                             memory_space=pltpu.SMEM)]
    first = 0
    for nt in tiles:
        in_specs.append(pl.BlockSpec((tm, D_MODEL), lambda j, first=first, nt=nt: (jnp.clip(j - first, 0, nt - 1), 0)))
        first += nt
    return pl.pallas_call(
        functools.partial(_dispatch_body, tm=tm, tiles=tiles, n_zero=n_zero, zero_steps=zero_steps), grid=(nsteps,),
        in_specs=in_specs,
        out_specs=pl.BlockSpec(memory_space=pl.ANY),
        out_shape=jax.ShapeDtypeStruct((n_slots, D_MODEL), F32),
        scratch_shapes=[pltpu.VMEM((2, tm // SUBLANES, SUBLANES, D_MODEL), F32),
                        pltpu.VMEM((SUBLANES, D_MODEL), F32), pltpu.SemaphoreType.DMA((2,))],
        compiler_params=_cparams(("arbitrary",)), name="dispatch")(
            dest3, unused.reshape(zero_steps, 1, n_zero), *xts)


XS_BUFFERS = 3


def _expert_body(be_ref, xs_hbm, wg_ref, wu_ref, wd_ref, y_ref, xbuf, sem, *, bm, nblk):
    del be_ref
    j = pl.program_id(0)
    ahead = XS_BUFFERS - 1

    def fetch(step):
        slot = step % XS_BUFFERS
        first = step * bm if isinstance(step, int) else pl.multiple_of(step * bm, bm)
        return pltpu.make_async_copy(xs_hbm.at[pl.ds(first, bm)], xbuf.at[slot], sem.at[slot])

    @pl.when(j == 0)
    def _():
        for s in range(min(ahead, nblk)):
            fetch(s).start()

    @pl.when(j + ahead < nblk)
    def _():
        fetch(j + ahead).start()

    fetch(j).wait()
    xb = xbuf[j % XS_BUFFERS].astype(BF16)
    gate = jnp.dot(xb, wg_ref[0], preferred_element_type=F32)
    hid = gate * jax.nn.sigmoid(gate) * jnp.dot(xb, wu_ref[0], preferred_element_type=F32)
    y_ref[...] = jnp.dot(hid.astype(BF16), wd_ref[0], preferred_element_type=F32)


def _experts(xs, block_e, wg, wu, wd, bm):
    nblk = block_e.shape[0]
    up_spec = pl.BlockSpec((1, D_MODEL, D_EXPERT), lambda j, be: (be[j], 0, 0))
    grid_spec = pltpu.PrefetchScalarGridSpec(
        num_scalar_prefetch=1, grid=(nblk,),
        in_specs=[pl.BlockSpec(memory_space=pl.ANY), up_spec, up_spec,
                  pl.BlockSpec((1, D_EXPERT, D_MODEL), lambda j, be: (be[j], 0, 0))],
        out_specs=pl.BlockSpec((bm, D_MODEL), lambda j, be: (j, 0)),
        scratch_shapes=[pltpu.VMEM((XS_BUFFERS, bm, D_MODEL), F32), pltpu.SemaphoreType.DMA((XS_BUFFERS,))])
    return pl.pallas_call(
        functools.partial(_expert_body, bm=bm, nblk=nblk), grid_spec=grid_spec,
        out_shape=jax.ShapeDtypeStruct((nblk * bm, D_MODEL), F32),
        compiler_params=_cparams(("arbitrary",)), name="experts")(block_e, xs, wg, wu, wd)


def _combine_body(d_cur, d_next, h_ref, route_ref, ys_hbm, y_ref, ybuf, sem, *, tm):
    j = pl.program_id(0)
    nblk = pl.num_programs(0)
    slot = j % 2

    def gather(idx_ref, s):
        def start_row(g, r):
            pltpu.make_async_copy(ys_hbm.at[pl.ds(idx_ref[0, g * SUBLANES + r], 1)], ybuf.at[s, g, pl.ds(r, 1)],
                                  sem.at[s]).start()
        _for_rows(2 * tm, start_row)

    @pl.when(j == 0)
    def _():
        gather(d_cur, 0)

    @pl.when(j + 1 < nblk)
    def _():
        gather(d_next, 1 - slot)

    _wait_rows(ys_hbm, 2 * tm, sem.at[slot])
    route = route_ref[...]
    w1 = route[:, 2:3]
    w2 = route[:, 3:4]
    ng = tm // SUBLANES
    y1 = ybuf[slot, 0:ng].reshape(tm, D_MODEL)
    y2 = ybuf[slot, ng:2 * ng].reshape(tm, D_MODEL)
    y_ref[...] = h_ref[...] + (y1 * w1 + y2 * w2)


def _tile_slots(dest, tm):
    nblk = dest.shape[1] // tm
    return dest.reshape(2, nblk, tm).transpose(1, 0, 2).reshape(nblk, 1, 2 * tm)


def _combine(h, route, ys, dest, tm):
    n = h.shape[0]
    nblk = n // tm
    d3 = _tile_slots(dest, tm)
    return pl.pallas_call(
        functools.partial(_combine_body, tm=tm), grid=(nblk,),
        in_specs=[
            pl.BlockSpec((None, 1, 2 * tm), lambda j: (j, 0, 0), memory_space=pltpu.SMEM),
            pl.BlockSpec((None, 1, 2 * tm), lambda j: (jnp.minimum(j + 1, nblk - 1), 0, 0), memory_space=pltpu.SMEM),
            pl.BlockSpec((tm, D_MODEL), lambda j: (j, 0)),
            pl.BlockSpec((tm, LANES), lambda j: (j, 0)),
            pl.BlockSpec(memory_space=pl.ANY),
        ],
        out_specs=pl.BlockSpec((tm, D_MODEL), lambda j: (j, 0)),
        out_shape=jax.ShapeDtypeStruct((n, D_MODEL), F32),
        scratch_shapes=[pltpu.VMEM((2, 2 * tm // SUBLANES, SUBLANES, D_MODEL), F32), pltpu.SemaphoreType.DMA((2,))],
        compiler_params=_cparams(("arbitrary",)), name="combine")(d3, d3, h, route, ys)


def _dispatch_plan(route, bm):
    n = route.shape[0]
    flat_e = jnp.concatenate([route[:, 0], route[:, 1]]).astype(jnp.int32)
    n_asg = flat_e.shape[0]
    onehot = (flat_e[:, None] == jnp.arange(N_EXPERTS, dtype=jnp.int32)[None, :]).astype(jnp.int32)
    csum = jnp.cumsum(onehot, axis=0)
    rank = jnp.sum(csum * onehot, axis=1) - 1
    counts = csum[-1]
    padded = (counts + bm - 1) // bm * bm
    pad_end = jnp.cumsum(padded)
    dest = jnp.sum((pad_end - padded)[None, :] * onehot, axis=1) + rank
    nblk = -(-n_asg // bm) + N_EXPERTS
    first_slot = jnp.arange(nblk, dtype=jnp.int32) * bm
    block_e = jnp.minimum(jnp.sum((pad_end[None, :] <= first_slot[:, None]).astype(jnp.int32), axis=1),
                          N_EXPERTS - 1)
    pad_cum = jnp.cumsum(padded - counts)
    i = jnp.arange(nblk * bm - n_asg, dtype=jnp.int32)
    seg = jnp.sum((pad_cum[None, :] <= i[:, None]).astype(jnp.int32), axis=1)
    seg_first_slot = jnp.concatenate([pad_end - padded + counts, pad_end[-1:]])
    seg_first_i = jnp.concatenate([jnp.zeros((1,), jnp.int32), pad_cum])
    unused = seg_first_slot[seg] + i - seg_first_i[seg]
    return dest.reshape(2, n), block_e, unused


def _hmoe(groups, wts, bm, tm):
    route_all = jnp.concatenate([route for _, _, route in groups], axis=0)
    dest_all, block_e, unused = _dispatch_plan(route_all, bm)
    xs = _dispatch([xt for _, xt, _ in groups], _tile_slots(dest_all, tm), unused, block_e.shape[0] * bm, tm)
    ys = _experts(xs, block_e, wts['wg'], wts['wu'], wts['wd'], bm)
    ys_out, off = [], 0
    for h, _, route in groups:
        n = h.shape[0]
        ys_out.append(_combine(h, route, ys, dest_all[:, off:off + n], tm))
        off += n
    return ys_out


def _prep_weights(norm_mix, w_in, qn_a, kn_a, qn_c, kn_c, gla_gate_up, gla_gate_bias, gla_norm, mem_norm,
                  w_mem_kv, w_branch_a, w_branch_b, w_branch_c, w_out, norm_ffn, w_router_group, b_router_group,
                  w_router_expert, b_router_expert, w_exp_gate, w_exp_up, w_exp_down):
    splits = (A_WIDTH, A_WIDTH, A_WIDTH, B_KW, B_KW, B_VW, B_VW, B_RANK, C_WIDTH, 3 * D_MODEL)
    offs = np.concatenate([[0], np.cumsum(splits)])
    names = ('wqa', 'wka', 'wva', 'wqb', 'wkb', 'wvb', 'wrb', 'wab', 'wqc', 'wgl')
    wts = {nm: w_in[:, int(offs[i]):int(offs[i + 1])].astype(BF16) for i, nm in enumerate(names)}
    wts['wab'] = jnp.pad(wts['wab'], ((0, 0), (0, LANES - B_RANK)))
    wts['gup'] = jnp.pad(gla_gate_up.astype(F32), ((0, LANES - B_RANK), (0, 0)))
    wts['gbias'] = gla_gate_bias.reshape(1, B_KW)
    wts['nm'] = norm_mix.reshape(1, D_MODEL)
    wts['qna'] = jnp.tile(qn_a, 2).reshape(1, LANES)
    wts['kna'] = jnp.tile(kn_a, 2).reshape(1, LANES)
    wts['qnc'] = qn_c.reshape(1, C_HEAD_DIM)
    wts['knc'] = kn_c.reshape(1, C_HEAD_DIM)
    wts['gn'] = gla_norm.reshape(1, B_DV)
    wts['mn'] = mem_norm.reshape(1, D_MODEL)
    wts['wmem'] = w_mem_kv.astype(BF16)
    wts['wba'] = w_branch_a.astype(BF16)
    wts['wbb'] = w_branch_b.astype(BF16)
    wts['wbc'] = w_branch_c.astype(BF16)
    wts['wout'] = w_out.astype(BF16)
    wts['nf'] = norm_ffn.reshape(1, D_MODEL)
    wr = jnp.concatenate([w_router_expert, w_router_group], axis=1)
    br = jnp.concatenate([b_router_expert, b_router_group])
    npad = LANES - N_EXPERTS - N_GROUPS_E
    wr = jnp.pad(wr, ((0, 0), (0, npad)))
    wr_hi = wr.astype(BF16)
    wts['wr'] = jnp.stack([wr_hi, (wr - wr_hi.astype(F32)).astype(BF16)])
    wts['br'] = jnp.pad(br, (0, npad)).reshape(1, LANES)
    wts['wg'] = w_exp_gate.astype(BF16)
    wts['wu'] = w_exp_up.astype(BF16)
    wts['wd'] = w_exp_down.astype(BF16)
    return wts


def _tile(n, want):
    t = min(n, want)
    assert n % t == 0
    return t


def _layer(x, wts, attend, gla_state, mem_kv, cfg):
    bsz, seq, _ = x.shape
    n = bsz * seq
    x2d = x.reshape(n, D_MODEL)
    qa, ka, va, qb, kb, vb, rb, la, qc = _in_proj(x2d, wts, _tile(n, cfg['tm_in']))

    def r3(a):
        return a.reshape(bsz, seq, a.shape[-1])

    def r4(a):
        return a.reshape(a.shape[0], bsz, seq, LANES)

    oa, extras = attend(r4(qa), r4(ka), r4(va))
    ob, s_fin = _gla(r3(qb), r3(kb), r3(vb), r3(la), r3(rb), wts['gn'], gla_state, cfg['gla_bb'],
                     _tile(seq, cfg['gla_tc']))
    oc = _cross(r3(qc), mem_kv, cfg['cross_bb'], _tile(seq, cfg['cross_tq']))
    pre_ffn = _merge(x2d, oa.reshape(n, A_GW), ob.reshape(n, B_VW), oc.reshape(n, C_WIDTH), wts,
                     _tile(n, cfg['tm_merge']))
    return pre_ffn, extras, s_fin


def kernel(x_prompt, x_sample, mem_prompt, cache_win1, cache_win2, cache_win3, state_gla, cache_mem, norm_mix, w_in, qn_a, kn_a, qn_c, kn_c, gla_gate_up, gla_gate_bias, gla_norm, mem_norm, w_mem_kv, w_branch_a, w_branch_b, w_branch_c, w_out, norm_ffn, w_router_group, b_router_group, w_router_expert, b_router_expert, w_exp_gate, w_exp_up, w_exp_down):
    wts = _prep_weights(norm_mix, w_in, qn_a, kn_a, qn_c, kn_c, gla_gate_up, gla_gate_bias, gla_norm, mem_norm,
                        w_mem_kv, w_branch_a, w_branch_b, w_branch_c, w_out, norm_ffn, w_router_group,
                        b_router_group, w_router_expert, b_router_expert, w_exp_gate, w_exp_up, w_exp_down)
    bsz, seq, _ = x_prompt.shape
    bd, n_new, _ = x_sample.shape

    mem_p = _mem_kv(mem_prompt.reshape(bsz * N_MEM, D_MODEL), wts['mn'], wts['wmem'], wts['knc'], 256)
    mem_p = mem_p.reshape(bsz, N_MEM * 2 * C_HEADS, C_HEAD_DIM)

    cfg_p = dict(tm_in=256, tm_merge=512, gla_bb=2, gla_tc=256, cross_bb=1, cross_tq=512)
    s0 = jnp.zeros((bsz, B_HEADS, B_DK, B_DV), F32)
    ffn_in_p, wins_p, gla_p = _layer(x_prompt, wts, _dilated_prompt, s0, mem_p, cfg_p)

    def attend_sample(qa, ka, va):
        return _dilated_sample(qa, ka, va, (cache_win1, cache_win2, cache_win3), bb=1)

    cfg_s = dict(tm_in=256, tm_merge=256, gla_bb=8, gla_tc=n_new, cross_bb=8, cross_tq=n_new)
    ffn_in_s, wins_s, gla_s = _layer(x_sample, wts, attend_sample, state_gla,
                                     cache_mem.reshape(bd, N_MEM * 2 * C_HEADS, C_HEAD_DIM), cfg_s)

    y_p, y_s = _hmoe([ffn_in_p, ffn_in_s], wts, bm=256, tm=256)
    y_p = y_p.reshape(x_prompt.shape)
    y_s = y_s.reshape(x_sample.shape)

    return (y_p, y_s, wins_p[0], wins_p[1], wins_p[2], gla_p, mem_p.reshape(bsz, N_MEM, 2, C_HEADS, C_HEAD_DIM),
            wins_s[0], wins_s[1], wins_s[2], gla_s)
```
